```python
import math
import numpy as np
import jax
import jax.numpy as jnp
from jax import lax

D_MODEL = 1024
BATCH = 16
SEQ = 4096
DEPTH = 1
DEC_BATCH = 128
DEC_SEQ = 4
PAST_LEN = 8192
PAGE_SIZE = 128

ML_HEADS = 4
ML_DK = 128
ML_DV = 128
ML_WIDTH = ML_HEADS * ML_DV
ML_CHUNK = 64
DA_HEADS = 4
DA_QK = 64
DA_V = 2 * DA_QK
DA_WIDTH = DA_HEADS * DA_V
DA_QBLOCK = 128
ROPE_THETA = 10000.0
PLE_DIM = 256
MOE_GROUPS = 4
MOE_PER_GROUP = 8
MOE_EXPERTS = MOE_GROUPS * MOE_PER_GROUP
MOE_TOP_K = 2
MOE_D_FF = 512
MOE_ROW_BLOCK = 128
NORM_EPS = 1e-6
SUBLN_EPS = 1e-5
IN_SIZES = (ML_HEADS * ML_DK, ML_HEADS * ML_DK, ML_WIDTH, ML_WIDTH, ML_HEADS, ML_HEADS,
            DA_HEADS * 2 * DA_QK, DA_HEADS * 2 * DA_QK, DA_WIDTH, D_MODEL, D_MODEL)
D_IN = sum(IN_SIZES)

kernel_name = "hybrid_mlstm_diffattn_hmoe_step"


def _rmsnorm(x, g, eps=NORM_EPS):
    xf = x.astype(jnp.float32)
    y = xf * lax.rsqrt(jnp.mean(xf * xf, axis=-1, keepdims=True) + eps)
    return (y * g.astype(jnp.float32)).astype(x.dtype)


def _split(z):
    idx = np.cumsum(IN_SIZES)[:-1].tolist()
    return jnp.split(z, idx, axis=-1)


def _rope(x, pos):
    half = x.shape[-1] // 2
    inv = ROPE_THETA ** (-jnp.arange(half, dtype=jnp.float32) / half)
    ang = pos.astype(jnp.float32)[:, None] * inv[None, :]
    ang = ang.reshape((1, ang.shape[0]) + (1,) * (x.ndim - 3) + (half,))
    cos, sin = jnp.cos(ang), jnp.sin(ang)
    xf = x.astype(jnp.float32)
    x1, x2 = xf[..., :half], xf[..., half:]
    return jnp.concatenate([x1 * cos - x2 * sin, x2 * cos + x1 * sin], axis=-1).astype(x.dtype)


def _mlstm_chunk(q, k, v, ig, lf, c0, n0, m0):
    L = q.shape[2]
    b = jnp.cumsum(lf, axis=-1)
    causal = jnp.tril(jnp.ones((L, L), dtype=bool))
    dmat = jnp.where(causal, b[..., :, None] - b[..., None, :] + ig[..., None, :], -jnp.inf)
    inter = b + m0[..., None]
    mt = jnp.maximum(inter, jnp.max(dmat, axis=-1))
    wts = jnp.exp(dmat - mt[..., None]) * jnp.einsum('bhtd,bhsd->bhts', q, k)
    decay0 = jnp.exp(inter - mt)
    num = jnp.einsum('bhts,bhse->bhte', wts, v) + decay0[..., None] * jnp.einsum('bhed,bhtd->bhte', c0, q)
    den = jnp.sum(wts, axis=-1) + decay0 * jnp.einsum('bhd,bhtd->bht', n0, q)
    h = num / jnp.maximum(jnp.abs(den), jnp.exp(-mt))[..., None]
    m_new = mt[..., -1]
    g_last = jnp.exp(b[..., -1] + m0 - m_new)
    w_s = jnp.exp(b[..., -1:] - b + ig - m_new[..., None])
    c_new = g_last[..., None, None] * c0 + jnp.einsum('bhs,bhse,bhsd->bhed', w_s, v, k)
    n_new = g_last[..., None] * n0 + jnp.einsum('bhs,bhsd->bhd', w_s, k)
    return h, (c_new, n_new, m_new)


def _mlstm_prompt(q, k, v, ig, lf):
    B, H, S, _ = q.shape
    L = min(ML_CHUNK, S)
    nc = S // L

    def chunks(a):
        return jnp.moveaxis(a.reshape(a.shape[:2] + (nc, L) + a.shape[3:]), 2, 0)

    init = (jnp.zeros((B, H, ML_DV, ML_DK), jnp.float32),
            jnp.zeros((B, H, ML_DK), jnp.float32),
            jnp.zeros((B, H), jnp.float32))

    def step(carry, xs):
        h, new = _mlstm_chunk(*xs, *carry)
        return new, h

    final, hs = lax.scan(step, init, (chunks(q), chunks(k), chunks(v), chunks(ig), chunks(lf)))
    h = jnp.moveaxis(hs, 0, 2).reshape(B, H, S, ML_DV)
    return h, final


def _diff_attn_prompt(q, k, v, lam):
    B, S, H, _, d = q.shape
    qb_len = min(DA_QBLOCK, S)
    nb = S // qb_len
    scale = d ** -0.5
    qb = jnp.moveaxis(q.reshape(B, nb, qb_len, H, 2, d), 1, 0)
    vf = v.astype(jnp.float32)
    kpos = jnp.arange(S)

    def blk(args):
        qi, bi = args
        s = jnp.einsum('bqhcd,bkhcd->bhcqk', qi, k).astype(jnp.float32) * scale
        qpos = bi * qb_len + jnp.arange(qb_len)
        s = jnp.where(kpos[None, :] <= qpos[:, None], s, -jnp.inf)
        p = jax.nn.softmax(s, axis=-1)
        a = p[:, :, 0] - lam * p[:, :, 1]
        return jnp.einsum('bhqk,bkhe->bqhe', a, vf)

    o = lax.map(blk, (qb, jnp.arange(nb)))
    return jnp.moveaxis(o, 0, 1).reshape(B, S, H, v.shape[-1])


def _diff_attn_sample(q, k, v, k_past, v_past, lam):
    T = q.shape[1]
    P = k_past.shape[1]
    scale = q.shape[-1] ** -0.5
    s_p = jnp.einsum('bthcd,bkhcd->bhctk', q, k_past).astype(jnp.float32) * scale
    s_n = jnp.einsum('bthcd,bshcd->bhcts', q, k).astype(jnp.float32) * scale
    s_n = jnp.where(jnp.tril(jnp.ones((T, T), dtype=bool)), s_n, -jnp.inf)
    p = jax.nn.softmax(jnp.concatenate([s_p, s_n], axis=-1), axis=-1)
    a = p[:, :, 0] - lam * p[:, :, 1]
    return (jnp.einsum('bhtk,bkhe->bthe', a[..., :P], v_past.astype(jnp.float32))
            + jnp.einsum('bhts,bshe->bthe', a[..., P:], v.astype(jnp.float32)))


def _hmoe(x, w_rg, b_rg, w_re, b_re, w_e_gate, w_e_up, w_e_down):
    T, D = x.shape
    K = MOE_TOP_K
    lg = (x @ w_rg + b_rg).astype(jnp.float32)
    pg = jax.nn.softmax(lg, axis=-1)
    g_star = jnp.argmax(lg, axis=-1)
    pg_top = jnp.take_along_axis(pg, g_star[:, None], axis=1)[:, 0]
    le = (x @ w_re + b_re).astype(jnp.float32).reshape(T, MOE_GROUPS, MOE_PER_GROUP)
    le_g = jnp.take_along_axis(le, g_star[:, None, None], axis=1)[:, 0]
    top_v, top_i = lax.top_k(jax.nn.softmax(le_g, axis=-1), K)
    top_v = top_v / jnp.sum(top_v, axis=-1, keepdims=True)
    expert_id = (g_star[:, None] * MOE_PER_GROUP + top_i).astype(jnp.int32)
    gate_w = pg_top[:, None] * top_v

    R = MOE_ROW_BLOCK
    n_slots = T * K
    n_blocks = -(-n_slots // R) + MOE_EXPERTS
    n_rows = n_blocks * R
    flat_e = expert_id.reshape(-1)
    flat_w = gate_w.reshape(-1)
    flat_t = jnp.arange(n_slots, dtype=jnp.int32) // K
    order = jnp.argsort(flat_e, stable=True)
    se = flat_e[order]
    counts = jnp.bincount(flat_e, length=MOE_EXPERTS)
    padded = ((counts + R - 1) // R) * R
    start = jnp.cumsum(counts) - counts
    pend = jnp.cumsum(padded)
    pstart = pend - padded
    dest = pstart[se] + jnp.arange(n_slots, dtype=jnp.int32) - start[se]
    row_tok = jnp.full((n_rows,), T, jnp.int32).at[dest].set(flat_t[order])
    row_w = jnp.zeros((n_rows,), x.dtype).at[dest].set(flat_w[order].astype(x.dtype))
    block_e = jnp.minimum(jnp.searchsorted(pend, jnp.arange(n_blocks) * R, side='right'),
                          MOE_EXPERTS - 1)
    x_pad = jnp.concatenate([x, jnp.zeros((1, D), x.dtype)], axis=0)

    def blk(args):
        toks, wts, e = args
        xb = x_pad[toks]
        hmid = jax.nn.silu(xb @ w_e_gate[e]) * (xb @ w_e_up[e])
        return (hmid @ w_e_down[e]) * wts[:, None]

    y_rows = lax.map(blk, (row_tok.reshape(n_blocks, R), row_w.reshape(n_blocks, R), block_e))
    out = jnp.zeros((T + 1, D), y_rows.dtype).at[row_tok].add(y_rows.reshape(n_rows, D))
    return out[:T].astype(x.dtype)


def _layer(x, ple, pos, li, past, w):
    (g_mix, w_in, b_ml_i, b_ml_f, lam_q1, lam_k1, lam_q2, lam_k2, g_sub, w_br_a, w_br_b, w_out,
     g_ffn, w_rg, b_rg, w_re, b_re, w_e_gate, w_e_up, w_e_down, g_ple, w_ple_gate, w_ple_proj) = w
    B, S, _ = x.shape
    f32 = jnp.float32
    xn = _rmsnorm(x, g_mix)
    mq, mk, mv, mo, mi, mf, aq, ak, av, ga, gb = _split(xn @ w_in)

    def heads(t, d):
        return jnp.moveaxis(t.reshape(B, S, ML_HEADS, d), 2, 1).astype(f32)
    q = heads(mq, ML_DK)
    k = heads(mk, ML_DK) * (ML_DK ** -0.5)
    v = heads(mv, ML_DV)
    ig = jnp.moveaxis((mi + b_ml_i).astype(f32), 2, 1)
    lf = jax.nn.log_sigmoid(jnp.moveaxis((mf + b_ml_f).astype(f32), 2, 1))

    qa = _rope(aq.reshape(B, S, DA_HEADS, 2, DA_QK), pos)
    ka = _rope(ak.reshape(B, S, DA_HEADS, 2, DA_QK), pos)
    va = av.reshape(B, S, DA_HEADS, DA_V)
    lam_init = 0.8 - 0.6 * math.exp(-0.3 * li)
    lam = (jnp.exp(jnp.sum(lam_q1.astype(f32) * lam_k1.astype(f32)))
           - jnp.exp(jnp.sum(lam_q2.astype(f32) * lam_k2.astype(f32))) + lam_init)

    if past is None:
        h, (c_new, n_new, m_new) = _mlstm_prompt(q, k, v, ig, lf)
        o = _diff_attn_prompt(qa, ka, va, lam)
    else:
        pool_k, pool_v, page_table, c0, n0, m0 = past
        h, (c_new, n_new, m_new) = _mlstm_chunk(q, k, v, ig, lf, c0.astype(f32), n0.astype(f32), m0.astype(f32))
        n_seq, n_pages = page_table.shape
        k_past = pool_k[page_table].reshape((n_seq, n_pages * pool_k.shape[1]) + pool_k.shape[2:])
        v_past = pool_v[page_table].reshape((n_seq, n_pages * pool_v.shape[1]) + pool_v.shape[2:])
        o = _diff_attn_sample(qa, ka, va, k_past, v_past, lam)

    h = jnp.moveaxis(h, 1, 2).reshape(B, S, ML_WIDTH).astype(x.dtype) * jax.nn.sigmoid(mo)
    branch_a = h @ w_br_a
    o = _rmsnorm(o, g_sub, SUBLN_EPS) * (1.0 - lam_init)
    branch_b = o.reshape(B, S, DA_WIDTH).astype(x.dtype) @ w_br_b
    mixed = jax.nn.sigmoid(ga) * branch_a + jax.nn.sigmoid(gb) * branch_b
    x = x + mixed @ w_out

    x = x + _hmoe(_rmsnorm(x, g_ffn).reshape(B * S, D_MODEL), w_rg, b_rg, w_re, b_re,
                  w_e_gate, w_e_up, w_e_down).reshape(B, S, D_MODEL)

    x = x + jax.nn.sigmoid(_rmsnorm(x, g_ple) @ w_ple_gate) * (ple @ w_ple_proj)
    return x, (ka, va, c_new, n_new, m_new)


def setup_inputs(seed: int = 0) -> dict:
    key = jax.random.key(seed)
    ks = iter(jax.random.split(key, 48))

    def nrm(shape, scale=1.0):
        return jax.random.normal(next(ks), shape, jnp.float32) * scale

    def gain(shape):
        return 1.0 + nrm(shape, 0.02)

    n_pages = PAST_LEN // PAGE_SIZE
    n_pool = (5 * DEC_BATCH * n_pages) // 4
    x_prompt = nrm((BATCH, SEQ, D_MODEL))
    x_sample = nrm((DEC_BATCH, DEC_SEQ, D_MODEL))
    cache_k = nrm((DEPTH, n_pool, PAGE_SIZE, DA_HEADS, 2, DA_QK))
    cache_v = nrm((DEPTH, n_pool, PAGE_SIZE, DA_HEADS, DA_V))
    state_mlstm_C = nrm((DEPTH, DEC_BATCH, ML_HEADS, ML_DV, ML_DK))
    state_mlstm_n = nrm((DEPTH, DEC_BATCH, ML_HEADS, ML_DK))
    state_mlstm_m = nrm((DEPTH, DEC_BATCH, ML_HEADS), 0.5)
    page_table = jax.random.permutation(next(ks), n_pool)[:DEC_BATCH * n_pages].reshape(
        DEC_BATCH, n_pages).astype(jnp.int32)
    p_prompt = nrm((DEPTH, BATCH, SEQ, PLE_DIM))
    p_sample = nrm((DEPTH, DEC_BATCH, DEC_SEQ, PLE_DIM))
    b_ml_f = jnp.broadcast_to(jnp.linspace(3.0, 6.0, ML_HEADS, dtype=jnp.float32),
                              (DEPTH, ML_HEADS)) + nrm((DEPTH, ML_HEADS), 0.1)
    return {
        "x_prompt": x_prompt,
        "x_sample": x_sample,
        "cache_k": cache_k,
        "cache_v": cache_v,
        "state_mlstm_C": state_mlstm_C,
        "state_mlstm_n": state_mlstm_n,
        "state_mlstm_m": state_mlstm_m,
        "page_table": page_table,
        "p_prompt": p_prompt,
        "p_sample": p_sample,
        "g_mix": gain((DEPTH, D_MODEL)),
        "w_in": nrm((DEPTH, D_MODEL, D_IN), D_MODEL ** -0.5),
        "b_ml_i": nrm((DEPTH, ML_HEADS), 0.1),
        "b_ml_f": b_ml_f,
        "lam_q1": nrm((DEPTH, DA_QK), 0.1),
        "lam_k1": nrm((DEPTH, DA_QK), 0.1),
        "lam_q2": nrm((DEPTH, DA_QK), 0.1),
        "lam_k2": nrm((DEPTH, DA_QK), 0.1),
        "g_sub": gain((DEPTH, DA_V)),
        "w_br_a": nrm((DEPTH, ML_WIDTH, D_MODEL), ML_WIDTH ** -0.5),
        "w_br_b": nrm((DEPTH, DA_WIDTH, D_MODEL), DA_WIDTH ** -0.5),
        "w_out": nrm((DEPTH, D_MODEL, D_MODEL), D_MODEL ** -0.5),
        "g_ffn": gain((DEPTH, D_MODEL)),
        "w_rg": nrm((DEPTH, D_MODEL, MOE_GROUPS), D_MODEL ** -0.5),
        "b_rg": nrm((DEPTH, MOE_GROUPS), 0.01),
        "w_re": nrm((DEPTH, D_MODEL, MOE_EXPERTS), D_MODEL ** -0.5),
        "b_re": nrm((DEPTH, MOE_EXPERTS), 0.01),
        "w_e_gate": nrm((DEPTH, MOE_EXPERTS, D_MODEL, MOE_D_FF), D_MODEL ** -0.5),
        "w_e_up": nrm((DEPTH, MOE_EXPERTS, D_MODEL, MOE_D_FF), D_MODEL ** -0.5),
        "w_e_down": nrm((DEPTH, MOE_EXPERTS, MOE_D_FF, D_MODEL), MOE_D_FF ** -0.5),
        "g_ple": gain((DEPTH, D_MODEL)),
        "w_ple_gate": nrm((DEPTH, D_MODEL, D_MODEL), D_MODEL ** -0.5),
        "w_ple_proj": nrm((DEPTH, PLE_DIM, D_MODEL), PLE_DIM ** -0.5),
        "g_final": gain((D_MODEL,)),
    }


def reference(x_prompt, x_sample, cache_k, cache_v, state_mlstm_C, state_mlstm_n, state_mlstm_m,
              page_table, p_prompt, p_sample, g_mix, w_in, b_ml_i, b_ml_f, lam_q1, lam_k1, lam_q2,
              lam_k2, g_sub, w_br_a, w_br_b, w_out, g_ffn, w_rg, b_rg, w_re, b_re, w_e_gate, w_e_up,
              w_e_down, g_ple, w_ple_gate, w_ple_proj, g_final):
    past_len = page_table.shape[1] * cache_k.shape[2]
    pos_p = jnp.arange(x_prompt.shape[1])
    pos_s = past_len + jnp.arange(x_sample.shape[1])
    hp, hs = x_prompt, x_sample
    kp_l, vp_l, cp_l, np_l, mp_l = [], [], [], [], []
    ks_l, vs_l, cs_l, ns_l, ms_l = [], [], [], [], []
    for li in range(DEPTH):
        w = (g_mix[li], w_in[li], b_ml_i[li], b_ml_f[li], lam_q1[li], lam_k1[li], lam_q2[li],
             lam_k2[li], g_sub[li], w_br_a[li], w_br_b[li], w_out[li], g_ffn[li], w_rg[li],
             b_rg[li], w_re[li], b_re[li], w_e_gate[li], w_e_up[li], w_e_down[li], g_ple[li],
             w_ple_gate[li], w_ple_proj[li])
        hp, (kp, vp, cp, n_p, mp) = _layer(hp, p_prompt[li], pos_p, li, None, w)
        past = (cache_k[li], cache_v[li], page_table, state_mlstm_C[li], state_mlstm_n[li],
                state_mlstm_m[li])
        hs, (k_s, v_s, c_s, n_s, m_s) = _layer(hs, p_sample[li], pos_s, li, past, w)
        kp_l.append(kp); vp_l.append(vp); cp_l.append(cp); np_l.append(n_p); mp_l.append(mp)
        ks_l.append(k_s); vs_l.append(v_s); cs_l.append(c_s); ns_l.append(n_s); ms_l.append(m_s)
    y_prompt = _rmsnorm(hp, g_final)
    y_sample = _rmsnorm(hs, g_final)
    return (y_prompt, y_sample,
            jnp.stack(kp_l), jnp.stack(vp_l), jnp.stack(cp_l), jnp.stack(np_l), jnp.stack(mp_l),
            jnp.stack(ks_l), jnp.stack(vs_l), jnp.stack(cs_l), jnp.stack(ns_l), jnp.stack(ms_l))
```

```python
import functools
import math

import jax
import jax.numpy as jnp
from jax import lax
from jax.experimental import pallas as pl
from jax.experimental.pallas import tpu as pltpu

F32 = jnp.float32
BF16 = jnp.bfloat16
I32 = jnp.int32

ML_HEADS = 4
ML_DK = 128
ML_DV = 128
DA_HEADS = 4
DA_QK = 64
DA_V = 128
ROPE_THETA = 10000.0
MOE_GROUPS = 4
MOE_PER_GROUP = 8
MOE_EXPERTS = MOE_GROUPS * MOE_PER_GROUP
MOE_TOP_K = 2
NORM_EPS = 1e-6
SUBLN_EPS = 1e-5
LAYER_INDEX = 0
LAM_INIT = 0.8 - 0.6 * math.exp(-0.3 * LAYER_INDEX)

LANES = 128
SUBLANES = 8
BF16_SUBLANES = 16
VMEM_LIMIT_BYTES = 56 * 1024 * 1024

TOKEN_TILE = 256
MLSTM_CHUNK = 256
ATTN_BLOCK = 512
PAGES_PER_STEP = 8
EXPERT_ROWS = 512
ROUTE_TILE = 512
ROUTER_ROWS = 48
SAMPLE_PAD = 16
NEW_KV_PAD = 128

NT_DIMS = (((1,), (1,)), ((), ()))
TN_DIMS = (((0,), (0,)), ((), ()))


def _params(*sem):
    return pltpu.CompilerParams(dimension_semantics=sem, vmem_limit_bytes=VMEM_LIMIT_BYTES)


def _sigmoid(x):
    return 1.0 / (1.0 + jnp.exp(-x))


def _log_sigmoid(x):
    return jnp.minimum(x, 0.0) - jnp.log1p(jnp.exp(-jnp.abs(x)))


def _rms(x, eps):
    return x * lax.rsqrt(jnp.mean(x * x, axis=-1, keepdims=True) + eps)


def _dot(a, b):
    return jnp.dot(a, b, preferred_element_type=F32)


def _dot_nt(a, b):
    return lax.dot_general(a, b, NT_DIMS, preferred_element_type=F32)


def _dot_tn(a, b):
    return lax.dot_general(a, b, TN_DIMS, preferred_element_type=F32)


def _split3(a):
    hi = a.astype(BF16)
    r1 = a - hi.astype(F32)
    mid = r1.astype(BF16)
    lo = (r1 - mid.astype(F32)).astype(BF16)
    return hi, mid, lo


def _inproj_kernel(xp_ref, xs_ref, g_ref, wm_ref, wgr_ref, wgc_ref, br_ref, bc_ref, cos_ref, sin_ref,
                   q_ref, k_ref, v_ref, og_ref, grow_ref, gcol_ref, qa_ref, kab_ref, vab_ref, sga_ref, sgb_ref,
                   kp_ref, vp_ref, ks_ref, vs_ref, *, n_prompt_tiles):
    i = pl.program_id(0)
    is_prompt = i < n_prompt_tiles
    x = jnp.where(is_prompt, xp_ref[...], xs_ref[...])
    xn = (_rms(x, NORM_EPS) * g_ref[...]).astype(BF16)

    def mm(lo, hi):
        return _dot(xn, wm_ref[:, lo:hi])

    w = ML_HEADS * ML_DK
    q_ref[...] = mm(0, w).astype(BF16)
    k_ref[...] = (mm(w, 2 * w) * (ML_DK ** -0.5)).astype(BF16)
    v_ref[...] = mm(2 * w, 3 * w).astype(BF16)
    og_ref[...] = _sigmoid(mm(3 * w, 4 * w)).astype(BF16)

    gr = _dot_nt(wgr_ref[...], xn) + br_ref[...]
    rr = lax.broadcasted_iota(I32, gr.shape, 0)
    grow_ref[...] = jnp.where(rr >= ML_HEADS, _log_sigmoid(gr), gr)
    gc = _dot(xn, wgc_ref[...]) + bc_ref[...]
    cc = lax.broadcasted_iota(I32, gc.shape, 1)
    gcol_ref[...] = jnp.where(cc >= ML_HEADS, _log_sigmoid(gc), gc)

    aw = DA_HEADS * 2 * DA_QK
    reps = aw // LANES
    cosv = jnp.concatenate([cos_ref[...]] * reps, axis=1)
    sinv = jnp.concatenate([sin_ref[...]] * reps, axis=1)
    half = DA_QK // 2

    def rope(z):
        lane = lax.broadcasted_iota(I32, z.shape, 1)
        upper = pltpu.roll(z, aw - half, axis=1)
        lower = pltpu.roll(z, half, axis=1)
        partner = jnp.where((lane % DA_QK) < half, upper, lower)
        return z * cosv + partner * sinv

    base = 4 * w
    qa_ref[...] = (rope(mm(base, base + aw)) * (DA_QK ** -0.5)).astype(BF16)
    ka = rope(mm(base + aw, base + 2 * aw))
    va = mm(base + 2 * aw, base + 3 * aw)
    kab_ref[...] = ka.astype(BF16)
    vab_ref[...] = va.astype(BF16)

    @pl.when(is_prompt)
    def _():
        kp_ref[...] = ka
        vp_ref[...] = va

    @pl.when(jnp.logical_not(is_prompt))
    def _():
        ks_ref[...] = ka
        vs_ref[...] = va

    base = base + 3 * aw
    d = g_ref.shape[-1]
    sga_ref[...] = _sigmoid(mm(base, base + d)).astype(BF16)
    sgb_ref[...] = _sigmoid(mm(base + d, base + 2 * d)).astype(BF16)


def _inproj(x_p, x_s, g_mix, w_main, w_gr, w_gc, b_row, b_col, cos_t, sin_t, tm, n_pos_tiles):
    tp, d = x_p.shape
    ts = x_s.shape[0]
    npt, nst = tp // tm, ts // tm
    t_all = tp + ts
    w = ML_HEADS * ML_DK
    aw = DA_HEADS * 2 * DA_QK
    ncols = w_main.shape[1]

    def tok(i):
        return (i, 0)

    def const(i):
        return (0, 0)

    def p_idx(i):
        return (jnp.minimum(i, npt - 1), 0)

    def s_idx(i):
        return (jnp.maximum(i - npt, 0), 0)

    def pos_idx(i):
        return (jnp.where(i < npt, i % n_pos_tiles, n_pos_tiles), 0)

    bf = lambda n: jax.ShapeDtypeStruct((t_all, n), BF16)
    out_shape = (bf(w), bf(w), bf(w), bf(w),
                 jax.ShapeDtypeStruct((BF16_SUBLANES, t_all), F32),
                 jax.ShapeDtypeStruct((t_all, LANES), F32),
                 bf(aw), bf(aw), bf(aw), bf(d), bf(d),
                 jax.ShapeDtypeStruct((tp, aw), F32), jax.ShapeDtypeStruct((tp, aw), F32),
                 jax.ShapeDtypeStruct((ts, aw), F32), jax.ShapeDtypeStruct((ts, aw), F32))
    out_specs = (pl.BlockSpec((tm, w), tok), pl.BlockSpec((tm, w), tok), pl.BlockSpec((tm, w), tok),
                 pl.BlockSpec((tm, w), tok),
                 pl.BlockSpec((BF16_SUBLANES, tm), lambda i: (0, i)),
                 pl.BlockSpec((tm, LANES), tok),
                 pl.BlockSpec((tm, aw), tok), pl.BlockSpec((tm, aw), tok), pl.BlockSpec((tm, aw), tok),
                 pl.BlockSpec((tm, d), tok), pl.BlockSpec((tm, d), tok),
                 pl.BlockSpec((tm, aw), p_idx), pl.BlockSpec((tm, aw), p_idx),
                 pl.BlockSpec((tm, aw), s_idx), pl.BlockSpec((tm, aw), s_idx))
    in_specs = [pl.BlockSpec((tm, d), p_idx), pl.BlockSpec((tm, d), s_idx),
                pl.BlockSpec((1, d), const),
                pl.BlockSpec((d, ncols), const),
                pl.BlockSpec((BF16_SUBLANES, d), const),
                pl.BlockSpec((d, LANES), const),
                pl.BlockSpec((BF16_SUBLANES, 1), const),
                pl.BlockSpec((1, LANES), const),
                pl.BlockSpec((tm, LANES), pos_idx), pl.BlockSpec((tm, LANES), pos_idx)]
    return pl.pallas_call(
        functools.partial(_inproj_kernel, n_prompt_tiles=npt),
        grid=(npt + nst,), in_specs=in_specs, out_specs=out_specs, out_shape=out_shape,
        compiler_params=_params("arbitrary"), name="inproj",
    )(x_p, x_s, g_mix, w_main, w_gr, w_gc, b_row, b_col, cos_t, sin_t)


def _mlstm_kernel(q_ref, k_ref, v_ref, og_ref, grow_ref, gcol_ref, c0_ref, n0_ref, m0_ref,
                  hg_ref, c_out, n_out, m_out, c_s, n_s, m_s, *, chunk):
    c = pl.program_id(1)
    nc = pl.num_programs(1)

    @pl.when(c == 0)
    def _():
        c_s[...] = c0_ref[...]
        n_s[...] = n0_ref[...]
        m_s[...] = m0_ref[...]

    L = chunk
    row = lax.broadcasted_iota(I32, (L, L), 0)
    col = lax.broadcasted_iota(I32, (L, L), 1)
    causal = col <= row
    tri = causal.astype(BF16)
    tri_t = (row <= col).astype(BF16)

    g_row = grow_ref[...]
    g_col = gcol_ref[...]
    cum_row = sum(_dot(p, tri_t) for p in _split3(g_row))
    cum_col = sum(_dot(tri, p) for p in _split3(g_col))

    for h in range(ML_HEADS):
        lo, hi = h * ML_DK, (h + 1) * ML_DK
        f = ML_HEADS + h
        b_col = cum_col[:, f:f + 1]
        ig_col = g_col[:, h:h + 1]
        b_row = cum_row[f:f + 1, :]
        ig_row = g_row[h:h + 1, :]
        m0 = m_s[h:h + 1, 0:1]
        c0 = c_s[h]
        n0 = n_s[h:h + 1, :]
        qh = q_ref[:, lo:hi]
        kh = k_ref[:, lo:hi]
        vh = v_ref[:, h * ML_DV:(h + 1) * ML_DV]

        dmat = jnp.where(causal, b_col - b_row + ig_row, -jnp.inf)
        inter = b_col + m0
        mt = jnp.maximum(inter, jnp.max(dmat, axis=-1, keepdims=True))
        wts = jnp.exp(dmat - mt) * _dot_nt(qh, kh)
        decay0 = jnp.exp(inter - mt)
        num = _dot(wts.astype(BF16), vh) + decay0 * _dot_nt(qh, c0.astype(BF16))
        qn = jnp.sum(qh.astype(F32) * n0, axis=-1, keepdims=True)
        den = jnp.sum(wts, axis=-1, keepdims=True) + decay0 * qn
        hh = num / jnp.maximum(jnp.abs(den), jnp.exp(-mt))
        hg_ref[:, h * ML_DV:(h + 1) * ML_DV] = (hh * og_ref[:, h * ML_DV:(h + 1) * ML_DV].astype(F32)).astype(BF16)

        b_last = b_col[L - 1:L, :]
        m_new = mt[L - 1:L, :]
        g_last = jnp.exp(b_last + m0 - m_new)
        ws = jnp.exp(b_last - b_col + ig_col - m_new)
        vw = (vh.astype(F32) * ws).astype(BF16)
        c_s[h] = g_last * c0 + _dot_tn(vw, kh)
        n_s[h:h + 1, :] = g_last * n0 + jnp.sum(kh.astype(F32) * ws, axis=0, keepdims=True)
        m_s[h:h + 1, :] = jnp.broadcast_to(m_new, (1, LANES))

    @pl.when(c == nc - 1)
    def _():
        c_out[...] = c_s[...]
        n_out[...] = n_s[...]
        m_out[...] = m_s[...]


def _mlstm(q, k, v, og, grow3, gcol, c0, n0, m0, *, batch, chunk, row_block_offset):
    nchunks_total = grow3.shape[0]
    nc = nchunks_total // batch
    w = ML_HEADS * ML_DK
    wv = ML_HEADS * ML_DV

    def tok(b, c):
        return (row_block_offset + b * nc + c, 0)

    def tok0(b, c):
        return (b * nc + c, 0)

    def st4(b, c):
        return (b, 0, 0, 0)

    def st3(b, c):
        return (b, 0, 0)

    rows = batch * nc * chunk
    out_shape = (jax.ShapeDtypeStruct((rows, wv), BF16),
                 jax.ShapeDtypeStruct(c0.shape, F32),
                 jax.ShapeDtypeStruct(n0.shape, F32),
                 jax.ShapeDtypeStruct(m0.shape, F32))
    in_specs = [pl.BlockSpec((chunk, w), tok), pl.BlockSpec((chunk, w), tok), pl.BlockSpec((chunk, wv), tok),
                pl.BlockSpec((chunk, wv), tok),
                pl.BlockSpec((None, BF16_SUBLANES, chunk), lambda b, c: (b * nc + c, 0, 0)),
                pl.BlockSpec((chunk, LANES), tok),
                pl.BlockSpec((None, ML_HEADS, ML_DV, ML_DK), st4),
                pl.BlockSpec((None, ML_HEADS, ML_DK), st3),
                pl.BlockSpec((None, ML_HEADS, LANES), st3)]
    out_specs = (pl.BlockSpec((chunk, wv), tok0),
                 pl.BlockSpec((None, ML_HEADS, ML_DV, ML_DK), st4),
                 pl.BlockSpec((None, ML_HEADS, ML_DK), st3),
                 pl.BlockSpec((None, ML_HEADS, LANES), st3))
    scratch = [pltpu.VMEM((ML_HEADS, ML_DV, ML_DK), F32), pltpu.VMEM((ML_HEADS, ML_DK), F32),
               pltpu.VMEM((ML_HEADS, LANES), F32)]
    return pl.pallas_call(
        functools.partial(_mlstm_kernel, chunk=chunk),
        grid=(batch, nc), in_specs=in_specs, out_specs=out_specs, out_shape=out_shape,
        scratch_shapes=scratch, compiler_params=_params("arbitrary", "arbitrary"), name="mlstm",
    )(q, k, v, og, grow3, gcol, c0, n0, m0)


def _lambda(lamv_ref):
    lv = lamv_ref[...]
    s1 = jnp.sum(lv[0:1, :] * lv[1:2, :], axis=-1, keepdims=True)
    s2 = jnp.sum(lv[2:3, :] * lv[3:4, :], axis=-1, keepdims=True)
    return jnp.exp(s1) - jnp.exp(s2) + LAM_INIT


def _subln(o, gsub_ref):
    return _rms(o, SUBLN_EPS) * gsub_ref[...] * (1.0 - LAM_INIT)


def _online_softmax_step(s, v, m_s, l_s, acc_s):
    m_prev = m_s[...]
    m_new = jnp.maximum(m_prev, jnp.max(s, axis=-1, keepdims=True))
    alpha = jnp.exp(m_prev - m_new)
    p = jnp.exp(s - m_new)
    l_s[...] = alpha * l_s[...] + jnp.sum(p, axis=-1, keepdims=True)
    acc_s[...] = alpha * acc_s[...] + _dot(p.astype(BF16), v)
    m_s[...] = m_new


def _attn_kernel(qt_ref, kt_ref, q_ref, k_ref, v_ref, lamv_ref, gsub_ref, o_ref, q2_s, m_s, l_s, acc_s, *, blk):
    p = pl.program_id(2)
    qi = qt_ref[p]
    ki = kt_ref[p]

    @pl.when(ki == 0)
    def _():
        q = q_ref[...]
        lane = lax.broadcasted_iota(I32, q.shape, 1)
        zero = jnp.zeros_like(q)
        q2_s[0:blk, :] = jnp.where(lane < DA_QK, q, zero)
        q2_s[blk:2 * blk, :] = jnp.where(lane >= DA_QK, q, zero)
        m_s[...] = jnp.full(m_s.shape, -jnp.inf, F32)
        l_s[...] = jnp.zeros(l_s.shape, F32)
        acc_s[...] = jnp.zeros(acc_s.shape, F32)

    s = _dot_nt(q2_s[...], k_ref[...])

    @pl.when(ki < qi)
    def _():
        _online_softmax_step(s, v_ref[...], m_s, l_s, acc_s)

    @pl.when(ki == qi)
    def _():
        r = lax.broadcasted_iota(I32, s.shape, 0) % blk
        cidx = lax.broadcasted_iota(I32, s.shape, 1)
        _online_softmax_step(jnp.where(cidx <= r, s, -jnp.inf), v_ref[...], m_s, l_s, acc_s)
        o2 = acc_s[...] / l_s[...]
        o = o2[0:blk, :] - _lambda(lamv_ref) * o2[blk:2 * blk, :]
        o_ref[...] = _subln(o, gsub_ref).astype(BF16)


def _attn_prompt(qa, kab, vab, lamv, gsub, *, batch, seq, blk):
    nq = seq // blk
    pairs = [(qi, ki) for qi in range(nq) for ki in range(qi + 1)]
    qt = jnp.asarray([p[0] for p in pairs], I32)
    kt = jnp.asarray([p[1] for p in pairs], I32)
    hw = 2 * DA_QK

    def q_idx(b, h, p, qt, kt):
        return (b * nq + qt[p], h)

    def k_idx(b, h, p, qt, kt):
        return (b * nq + kt[p], h)

    def const(b, h, p, qt, kt):
        return (0, 0)

    grid_spec = pltpu.PrefetchScalarGridSpec(
        num_scalar_prefetch=2, grid=(batch, DA_HEADS, len(pairs)),
        in_specs=[pl.BlockSpec((blk, hw), q_idx), pl.BlockSpec((blk, hw), k_idx), pl.BlockSpec((blk, DA_V), k_idx),
                  pl.BlockSpec(lamv.shape, const), pl.BlockSpec((1, DA_V), const)],
        out_specs=pl.BlockSpec((blk, DA_V), q_idx),
        scratch_shapes=[pltpu.VMEM((2 * blk, hw), BF16), pltpu.VMEM((2 * blk, 1), F32),
                        pltpu.VMEM((2 * blk, 1), F32), pltpu.VMEM((2 * blk, DA_V), F32)])
    return pl.pallas_call(
        functools.partial(_attn_kernel, blk=blk), grid_spec=grid_spec,
        out_shape=jax.ShapeDtypeStruct((batch * seq, DA_HEADS * DA_V), BF16),
        compiler_params=_params("arbitrary", "arbitrary", "arbitrary"), name="attn_prompt",
    )(qt, kt, qa, kab, vab, lamv, gsub)


def _dec_attn_kernel(pt_ref, qbd_ref, *refs, pages, n_new):
    k_refs = refs[:pages]
    v_refs = refs[pages:2 * pages]
    kn_ref, vn_ref, lamv_ref, gsub_ref, o_ref, m_s, l_s, acc_s = refs[2 * pages:]
    j = pl.program_id(1)
    nj = pl.num_programs(1)

    @pl.when(j == 0)
    def _():
        m_s[...] = jnp.full(m_s.shape, -jnp.inf, F32)
        l_s[...] = jnp.zeros(l_s.shape, F32)
        acc_s[...] = jnp.zeros(acc_s.shape, F32)

    q = qbd_ref[...]
    s = jnp.concatenate([_dot_nt(q, k_refs[p][...].astype(BF16)) for p in range(pages)], axis=1)
    m_prev = m_s[...]
    m_new = jnp.maximum(m_prev, jnp.max(s, axis=-1, keepdims=True))
    alpha = jnp.exp(m_prev - m_new)
    pr = jnp.exp(s - m_new)
    page = k_refs[0].shape[0]
    pv = sum(_dot(pr[:, p * page:(p + 1) * page].astype(BF16), v_refs[p][...].astype(BF16)) for p in range(pages))
    l_s[...] = alpha * l_s[...] + jnp.sum(pr, axis=-1, keepdims=True)
    acc_s[...] = alpha * acc_s[...] + pv
    m_s[...] = m_new

    @pl.when(j == nj - 1)
    def _():
        sn = _dot_nt(q, kn_ref[...])
        t = jnp.minimum(lax.broadcasted_iota(I32, sn.shape, 0) % SUBLANES, n_new - 1)
        cidx = lax.broadcasted_iota(I32, sn.shape, 1)
        _online_softmax_step(jnp.where(cidx <= t, sn, -jnp.inf), vn_ref[...], m_s, l_s, acc_s)
        o2 = acc_s[...] / l_s[...]
        lam = _lambda(lamv_ref)
        outs = []
        for h in range(DA_HEADS):
            r0 = h * 2 * SUBLANES
            o0 = o2[r0:r0 + SUBLANES, h * DA_V:(h + 1) * DA_V]
            o1 = o2[r0 + SUBLANES:r0 + 2 * SUBLANES, h * DA_V:(h + 1) * DA_V]
            outs.append(_subln(o0 - lam * o1, gsub_ref))
        o_ref[...] = jnp.concatenate(outs, axis=1)


def _attn_sample(page_table, qbd, cache_k, cache_v, kn, vn, lamv, gsub, *, n_new):
    bs, npg = page_table.shape
    pages = min(PAGES_PER_STEP, npg)
    while npg % pages:
        pages -= 1
    page, width = cache_k.shape[1], cache_k.shape[2]
    rows = qbd.shape[1]

    def page_spec(p):
        return pl.BlockSpec((None, page, width), lambda b, j, pt: (pt[b, j * pages + p], 0, 0))

    def seq3(b, j, pt):
        return (b, 0, 0)

    def const(b, j, pt):
        return (0, 0)

    in_specs = ([pl.BlockSpec((None, rows, width), seq3)]
                + [page_spec(p) for p in range(pages)] + [page_spec(p) for p in range(pages)]
                + [pl.BlockSpec((None, NEW_KV_PAD, width), seq3), pl.BlockSpec((None, NEW_KV_PAD, width), seq3),
                   pl.BlockSpec(lamv.shape, const), pl.BlockSpec((1, DA_V), const)])
    grid_spec = pltpu.PrefetchScalarGridSpec(
        num_scalar_prefetch=1, grid=(bs, npg // pages), in_specs=in_specs,
        out_specs=pl.BlockSpec((None, SUBLANES, width), seq3),
        scratch_shapes=[pltpu.VMEM((rows, 1), F32), pltpu.VMEM((rows, 1), F32), pltpu.VMEM((rows, width), F32)])
    return pl.pallas_call(
        functools.partial(_dec_attn_kernel, pages=pages, n_new=n_new), grid_spec=grid_spec,
        out_shape=jax.ShapeDtypeStruct((bs, SUBLANES, width), F32),
        compiler_params=_params("arbitrary", "arbitrary"), name="attn_sample",
    )(page_table, qbd, *([cache_k] * pages), *([cache_v] * pages), kn, vn, lamv, gsub)


def _merge_kernel(xp_ref, xs_ref, hgp_ref, hgs_ref, op_ref, os_ref, sga_ref, sgb_ref, wa_ref, wb_ref, wo_ref,
                  gffn_ref, wr_ref, br_ref, x1_ref, lg_ref, *, n_prompt_tiles):
    is_prompt = pl.program_id(0) < n_prompt_tiles
    x = jnp.where(is_prompt, xp_ref[...], xs_ref[...])
    hg = jnp.where(is_prompt, hgp_ref[...], hgs_ref[...])
    o = jnp.where(is_prompt, op_ref[...], os_ref[...])
    mixed = sga_ref[...].astype(F32) * _dot(hg, wa_ref[...]) + sgb_ref[...].astype(F32) * _dot(o, wb_ref[...])
    x1 = x + _dot(mixed.astype(BF16), wo_ref[...])
    x1_ref[...] = x1
    xn = (_rms(x1, NORM_EPS) * gffn_ref[...]).astype(BF16)
    lg_ref[...] = _dot_nt(wr_ref[...], xn) + br_ref[...]


def _merge(x_p, x_s, hg_p, hg_s, o_p, o_s, sga, sgb, wa, wb, wo, g_ffn, wr, br, tm):
    tp, d = x_p.shape
    ts = x_s.shape[0]
    npt, nst = tp // tm, ts // tm
    t_all = tp + ts

    def tok(i):
        return (i, 0)

    def const(i):
        return (0, 0)

    def p_idx(i):
        return (jnp.minimum(i, npt - 1), 0)

    def s_idx(i):
        return (jnp.maximum(i - npt, 0), 0)

    wv, wo_in = hg_p.shape[1], o_p.shape[1]
    in_specs = [pl.BlockSpec((tm, d), p_idx), pl.BlockSpec((tm, d), s_idx),
                pl.BlockSpec((tm, wv), p_idx), pl.BlockSpec((tm, wv), s_idx),
                pl.BlockSpec((tm, wo_in), p_idx), pl.BlockSpec((tm, wo_in), s_idx),
                pl.BlockSpec((tm, d), tok), pl.BlockSpec((tm, d), tok),
                pl.BlockSpec(wa.shape, const), pl.BlockSpec(wb.shape, const), pl.BlockSpec(wo.shape, const),
                pl.BlockSpec((1, d), const), pl.BlockSpec(wr.shape, const), pl.BlockSpec((ROUTER_ROWS, 1), const)]
    out_shape = (jax.ShapeDtypeStruct((t_all, d), F32), jax.ShapeDtypeStruct((ROUTER_ROWS, t_all), F32))
    out_specs = (pl.BlockSpec((tm, d), tok), pl.BlockSpec((ROUTER_ROWS, tm), lambda i: (0, i)))
    return pl.pallas_call(
        functools.partial(_merge_kernel, n_prompt_tiles=npt),
        grid=(npt + nst,), in_specs=in_specs, out_specs=out_specs, out_shape=out_shape,
        compiler_params=_params("arbitrary"), name="merge",
    )(x_p, x_s, hg_p, hg_s, o_p, o_s, sga, sgb, wa, wb, wo, g_ffn, wr, br)


def _route_kernel(lg_ref, eid_ref, gw_ref):
    x = lg_ref[...]
    sub = lax.broadcasted_iota(I32, (SUBLANES, x.shape[1]), 0)
    lg = jnp.where(sub < MOE_GROUPS, x[0:SUBLANES, :], -jnp.inf)
    gmax = jnp.max(lg, axis=0, keepdims=True)
    g_star = jnp.min(jnp.where(lg == gmax, sub, SUBLANES), axis=0, keepdims=True)
    pg_top = 1.0 / jnp.sum(jnp.exp(lg - gmax), axis=0, keepdims=True)
    le = x[SUBLANES:2 * SUBLANES, :]
    for g in range(1, MOE_GROUPS):
        le = jnp.where(g_star == g, x[(g + 1) * SUBLANES:(g + 2) * SUBLANES, :], le)
    ex = jnp.exp(le - jnp.max(le, axis=0, keepdims=True))
    pe = ex / jnp.sum(ex, axis=0, keepdims=True)
    v1 = jnp.max(pe, axis=0, keepdims=True)
    i1 = jnp.min(jnp.where(pe == v1, sub, SUBLANES), axis=0, keepdims=True)
    rest = jnp.where(sub == i1, -jnp.inf, pe)
    v2 = jnp.max(rest, axis=0, keepdims=True)
    i2 = jnp.min(jnp.where(rest == v2, sub, SUBLANES), axis=0, keepdims=True)
    tot = v1 + v2
    e1 = g_star * MOE_PER_GROUP + i1
    e2 = g_star * MOE_PER_GROUP + i2
    w1 = pg_top * (v1 / tot)
    w2 = pg_top * (v2 / tot)
    eid_ref[...] = jnp.where(sub == 0, e1, jnp.where(sub == 1, e2, 0))
    gw_ref[...] = jnp.where(sub == 0, w1, jnp.where(sub == 1, w2, 0.0))


def _route(lg):
    rows, t_all = lg.shape
    tb = ROUTE_TILE
    while t_all % tb:
        tb //= 2
    spec = pl.BlockSpec((SUBLANES, tb), lambda i: (0, i))
    return pl.pallas_call(
        _route_kernel, grid=(t_all // tb,),
        in_specs=[pl.BlockSpec((rows, tb), lambda i: (0, i))], out_specs=(spec, spec),
        out_shape=(jax.ShapeDtypeStruct((SUBLANES, t_all), I32), jax.ShapeDtypeStruct((SUBLANES, t_all), F32)),
        compiler_params=_params("arbitrary"), name="route",
    )(lg)


def _expert_kernel(be_ref, nb_ref, x_ref, w_ref, gffn_ref, wg_ref, wu_ref, wd_ref, y_ref, wg_s, wu_s, wd_s):
    b = pl.program_id(0)
    changed = jnp.logical_or(b == 0, be_ref[b] != be_ref[jnp.maximum(b - 1, 0)])

    @pl.when(jnp.logical_and(b < nb_ref[0], changed))
    def _():
        wg_s[...] = wg_ref[...].astype(BF16)
        wu_s[...] = wu_ref[...].astype(BF16)
        wd_s[...] = wd_ref[...].astype(BF16)

    @pl.when(b < nb_ref[0])
    def _():
        xn = (_rms(x_ref[...], NORM_EPS) * gffn_ref[...]).astype(BF16)
        g = _dot(xn, wg_s[...])
        u = _dot(xn, wu_s[...])
        hmid = (g * _sigmoid(g) * u).astype(BF16)
        y_ref[...] = _dot(hmid, wd_s[...]) * w_ref[...]

    @pl.when(b >= nb_ref[0])
    def _():
        y_ref[...] = jnp.zeros(y_ref.shape, F32)


def _experts(block_e, n_used, x_rows, w_rows, g_ffn, w_gate, w_up, w_down, rows):
    n_rows, d = x_rows.shape
    nb = n_rows // rows
    ff = w_gate.shape[-1]

    def row_idx(b, be, nu):
        return (b, 0)

    def const(b, be, nu):
        return (0, 0)

    def w_idx(b, be, nu):
        return (be[b], 0, 0)

    grid_spec = pltpu.PrefetchScalarGridSpec(
        num_scalar_prefetch=2, grid=(nb,),
        in_specs=[pl.BlockSpec((rows, d), row_idx), pl.BlockSpec((rows, 1), row_idx), pl.BlockSpec((1, d), const),
                  pl.BlockSpec((None, d, ff), w_idx), pl.BlockSpec((None, d, ff), w_idx),
                  pl.BlockSpec((None, ff, d), w_idx)],
        out_specs=pl.BlockSpec((rows, d), row_idx),
        scratch_shapes=[pltpu.VMEM((d, ff), BF16), pltpu.VMEM((d, ff), BF16), pltpu.VMEM((ff, d), BF16)])
    return pl.pallas_call(
        _expert_kernel, grid_spec=grid_spec, out_shape=jax.ShapeDtypeStruct((n_rows, d), F32),
        compiler_params=_params("arbitrary"), name="experts",
    )(block_e, n_used, x_rows, w_rows, g_ffn, w_gate, w_up, w_down)


def _final_kernel(x1_ref, moe_ref, plep_ref, ples_ref, gple_ref, wpg_ref, wpp_ref, gfin_ref, yp_ref, ys_ref,
                  *, n_prompt_tiles):
    is_prompt = pl.program_id(0) < n_prompt_tiles
    x2 = x1_ref[...] + moe_ref[...]
    xn = (_rms(x2, NORM_EPS) * gple_ref[...]).astype(BF16)
    ple = jnp.where(is_prompt, plep_ref[...], ples_ref[...]).astype(BF16)
    x3 = x2 + _sigmoid(_dot(xn, wpg_ref[...])) * _dot(ple, wpp_ref[...])
    y = _rms(x3, NORM_EPS) * gfin_ref[...]

    @pl.when(is_prompt)
    def _():
        yp_ref[...] = y

    @pl.when(jnp.logical_not(is_prompt))
    def _():
        ys_ref[...] = y


def _final(x1, moe, ple_p, ple_s, g_ple, wpg, wpp, g_final, tm):
    t_all, d = x1.shape
    tp, ts = ple_p.shape[0], ple_s.shape[0]
    npt, nst = tp // tm, ts // tm
    pd = ple_p.shape[1]

    def tok(i):
        return (i, 0)

    def const(i):
        return (0, 0)

    def p_idx(i):
        return (jnp.minimum(i, npt - 1), 0)

    def s_idx(i):
        return (jnp.maximum(i - npt, 0), 0)

    in_specs = [pl.BlockSpec((tm, d), tok), pl.BlockSpec((tm, d), tok),
                pl.BlockSpec((tm, pd), p_idx), pl.BlockSpec((tm, pd), s_idx),
                pl.BlockSpec((1, d), const), pl.BlockSpec(wpg.shape, const), pl.BlockSpec(wpp.shape, const),
                pl.BlockSpec((1, d), const)]
    return pl.pallas_call(
        functools.partial(_final_kernel, n_prompt_tiles=npt),
        grid=(npt + nst,), in_specs=in_specs,
        out_specs=(pl.BlockSpec((tm, d), p_idx), pl.BlockSpec((tm, d), s_idx)),
        out_shape=(jax.ShapeDtypeStruct((tp, d), F32), jax.ShapeDtypeStruct((ts, d), F32)),
        compiler_params=_params("arbitrary"), name="final",
    )(x1, moe, ple_p, ple_s, g_ple, wpg, wpp, g_final)


def _rope_tables(pos):
    half = DA_QK // 2
    inv = ROPE_THETA ** (-jnp.arange(half, dtype=F32) / half)
    ang = pos.astype(F32)[:, None] * inv[None, :]
    cos, sin = jnp.cos(ang), jnp.sin(ang)
    reps = LANES // DA_QK
    cos_t = jnp.tile(jnp.concatenate([cos, cos], axis=1), (1, reps))
    sin_t = jnp.tile(jnp.concatenate([-sin, sin], axis=1), (1, reps))
    return cos_t, sin_t


def _tile(limit, *sizes):
    t = limit
    while any(s % t for s in sizes):
        t //= 2
    return t


def kernel(x_prompt, x_sample, cache_k, cache_v, state_mlstm_C, state_mlstm_n, state_mlstm_m, page_table, p_prompt, p_sample, g_mix, w_in, b_ml_i, b_ml_f, lam_q1, lam_k1, lam_q2, lam_k2, g_sub, w_br_a, w_br_b, w_out, g_ffn, w_rg, b_rg, w_re, b_re, w_e_gate, w_e_up, w_e_down, g_ple, w_ple_gate, w_ple_proj, g_final):
    depth = w_in.shape[0]
    assert depth == 1, "single-layer step"
    bp, sp, d = x_prompt.shape
    bs, ss, _ = x_sample.shape
    assert ss <= SUBLANES
    tp, ts = bp * sp, bs * ss
    t_all = tp + ts
    n_pages, page = page_table.shape[1], cache_k.shape[2]
    past_len = n_pages * page
    w = ML_HEADS * ML_DK
    aw = DA_HEADS * 2 * DA_QK
    li = 0

    wi = w_in[li]
    sizes = (w, w, ML_HEADS * ML_DV, ML_HEADS * ML_DV, ML_HEADS, ML_HEADS, aw, aw, DA_HEADS * DA_V, d, d)
    edges = [0]
    for n in sizes:
        edges.append(edges[-1] + n)
    assert edges[-1] == wi.shape[1]
    seg = [wi[:, edges[i]:edges[i + 1]] for i in range(11)]
    w_main = jnp.concatenate(seg[0:4] + seg[6:11], axis=1).astype(BF16)
    w_gates = jnp.concatenate([seg[4], seg[5]], axis=1)
    w_gr = jnp.pad(w_gates.T, ((0, BF16_SUBLANES - 2 * ML_HEADS), (0, 0))).astype(BF16)
    w_gc = jnp.pad(w_gates, ((0, 0), (0, LANES - 2 * ML_HEADS))).astype(BF16)
    b_gates = jnp.concatenate([b_ml_i[li], b_ml_f[li]]).astype(F32)
    b_row = jnp.pad(b_gates, (0, BF16_SUBLANES - 2 * ML_HEADS))[:, None]
    b_col = jnp.pad(b_gates, (0, LANES - 2 * ML_HEADS))[None, :]
    lamv = jnp.stack([lam_q1[li], lam_k1[li], lam_q2[li], lam_k2[li]]).astype(F32)
    gsub = g_sub[li][None, :].astype(F32)
    w_router = jnp.zeros((ROUTER_ROWS, d), F32).at[0:MOE_GROUPS].set(w_rg[li].T).at[SUBLANES:SUBLANES + MOE_EXPERTS].set(w_re[li].T)
    b_router = jnp.zeros((ROUTER_ROWS,), F32).at[0:MOE_GROUPS].set(b_rg[li]).at[SUBLANES:SUBLANES + MOE_EXPERTS].set(b_re[li])

    tm = _tile(TOKEN_TILE, sp, ts)
    cos_p, sin_p = _rope_tables(jnp.arange(sp))
    cos_s, sin_s = _rope_tables(past_len + (jnp.arange(tm) % ss))
    cos_t = jnp.concatenate([cos_p, cos_s], axis=0)
    sin_t = jnp.concatenate([sin_p, sin_s], axis=0)
    xp2 = x_prompt.reshape(tp, d)
    xs2 = x_sample.reshape(ts, d)
    (q_ml, k_ml, v_ml, og, grow, gcol, qa, kab, vab, sga, sgb, k_p, v_p, k_s, v_s) = _inproj(
        xp2, xs2, g_mix[li][None, :], w_main, w_gr, w_gc, b_row, b_col, cos_t, sin_t, tm, sp // tm)

    chunk = _tile(MLSTM_CHUNK, sp)
    ncp = sp // chunk
    grow3_p = grow[:, :tp].reshape(BF16_SUBLANES, bp * ncp, chunk).transpose(1, 0, 2)
    zc = jnp.zeros((bp, ML_HEADS, ML_DV, ML_DK), F32)
    zn = jnp.zeros((bp, ML_HEADS, ML_DK), F32)
    zm = jnp.zeros((bp, ML_HEADS, LANES), F32)
    hg_p, c_p, n_p, m_p = _mlstm(q_ml, k_ml, v_ml, og, grow3_p, gcol, zc, zn, zm,
                                 batch=bp, chunk=chunk, row_block_offset=0)

    padn = SAMPLE_PAD - ss

    def pad_seq(a):
        return jnp.pad(a[tp:].reshape(bs, ss, -1), ((0, 0), (0, padn), (0, 0))).reshape(bs * SAMPLE_PAD, -1)

    neutral = jnp.where(jnp.arange(LANES) < ML_HEADS, -1e30, 0.0).astype(F32)
    gcol_s = jnp.concatenate([gcol[tp:].reshape(bs, ss, LANES),
                              jnp.broadcast_to(neutral, (bs, padn, LANES))], axis=1).reshape(bs * SAMPLE_PAD, LANES)
    grow3_s = jnp.concatenate([grow[:, tp:].reshape(BF16_SUBLANES, bs, ss).transpose(1, 0, 2),
                               jnp.broadcast_to(neutral[:BF16_SUBLANES, None], (bs, BF16_SUBLANES, padn))], axis=2)
    m0_s = jnp.broadcast_to(state_mlstm_m[li].astype(F32)[:, :, None], (bs, ML_HEADS, LANES))
    hg_s_pad, c_s, n_s, m_s = _mlstm(pad_seq(q_ml), pad_seq(k_ml), pad_seq(v_ml), pad_seq(og), grow3_s, gcol_s,
                                     state_mlstm_C[li].astype(F32), state_mlstm_n[li].astype(F32), m0_s,
                                     batch=bs, chunk=SAMPLE_PAD, row_block_offset=0)
    hg_s = hg_s_pad.reshape(bs, SAMPLE_PAD, -1)[:, :ss].reshape(ts, -1)

    blk = _tile(ATTN_BLOCK, sp)
    o_p = _attn_prompt(qa, kab, vab, lamv, gsub, batch=bp, seq=sp, blk=blk)

    q_s = jnp.pad(qa[tp:].reshape(bs, ss, aw), ((0, 0), (0, SUBLANES - ss), (0, 0)))
    ngrp = aw // DA_QK
    lane_grp = jnp.arange(aw) // DA_QK
    qbd = jnp.where(lane_grp[None, None, None, :] == jnp.arange(ngrp)[None, :, None, None],
                    q_s[:, None, :, :], jnp.zeros((), BF16)).reshape(bs, ngrp * SUBLANES, aw)
    kn = jnp.pad(kab[tp:].reshape(bs, ss, aw), ((0, 0), (0, NEW_KV_PAD - ss), (0, 0)))
    vn = jnp.pad(vab[tp:].reshape(bs, ss, aw), ((0, 0), (0, NEW_KV_PAD - ss), (0, 0)))
    ck = cache_k[li].reshape(cache_k.shape[1], page, aw)
    cv = cache_v[li].reshape(cache_v.shape[1], page, aw)
    o_s = _attn_sample(page_table, qbd, ck, cv, kn, vn, lamv, gsub, n_new=ss)[:, :ss].reshape(ts, aw).astype(BF16)

    x1, lg = _merge(xp2, xs2, hg_p, hg_s, o_p, o_s, sga, sgb,
                    w_br_a[li].astype(BF16), w_br_b[li].astype(BF16), w_out[li].astype(BF16),
                    g_ffn[li][None, :], w_router.astype(BF16), b_router[:, None], tm)

    eid8, gw8 = _route(lg)
    rows = EXPERT_ROWS
    while rows > SUBLANES and rows * MOE_EXPERTS > 2 * t_all * MOE_TOP_K:
        rows //= 2
    n_slots = t_all * MOE_TOP_K
    nb = -(-n_slots // rows) + MOE_EXPERTS
    n_rows = nb * rows
    flat_e = eid8[:MOE_TOP_K].T.reshape(-1)
    flat_w = gw8[:MOE_TOP_K].T.reshape(-1)
    order = jnp.argsort(flat_e, stable=True).astype(I32)
    se = flat_e[order]
    counts = jnp.bincount(flat_e, length=MOE_EXPERTS).astype(I32)
    padded = ((counts + rows - 1) // rows) * rows
    start = jnp.cumsum(counts) - counts
    pend = jnp.cumsum(padded)
    pstart = pend - padded
    dest = pstart[se] + jnp.arange(n_slots, dtype=I32) - start[se]
    row_tok = jnp.zeros((n_rows,), I32).at[dest].set(order // MOE_TOP_K)
    row_w = jnp.zeros((n_rows,), F32).at[dest].set(flat_w[order])
    block_e = jnp.minimum(jnp.searchsorted(pend, jnp.arange(nb, dtype=I32) * rows, side='right'),
                          MOE_EXPERTS - 1).astype(I32)
    n_used = (pend[-1:] // rows).astype(I32)
    slot_pos = jnp.zeros((n_slots,), I32).at[order].set(dest)

    y_rows = _experts(block_e, n_used, x1[row_tok], row_w[:, None], g_ffn[li][None, :],
                      w_e_gate[li], w_e_up[li], w_e_down[li], rows)
    moe = y_rows[slot_pos[0::2]] + y_rows[slot_pos[1::2]]

    pd = p_prompt.shape[-1]
    y_p, y_s = _final(x1, moe, p_prompt[li].reshape(tp, pd), p_sample[li].reshape(ts, pd), g_ple[li][None, :],
                      w_ple_gate[li].astype(BF16), w_ple_proj[li].astype(BF16), g_final[None, :], tm)

    return (y_p.reshape(bp, sp, d), y_s.reshape(bs, ss, d),
            k_p.reshape(1, bp, sp, DA_HEADS, 2, DA_QK), v_p.reshape(1, bp, sp, DA_HEADS, DA_V),
            c_p[None], n_p[None], m_p[None, :, :, 0],
            k_s.reshape(1, bs, ss, DA_HEADS, 2, DA_QK), v_s.reshape(1, bs, ss, DA_HEADS, DA_V),
            c_s[None], n_s[None], m_s[None, :, :, 0])
```

```python
import functools
import math

import jax
import jax.numpy as jnp
from jax import lax
from jax.experimental import pallas as pl
from jax.experimental.pallas import tpu as pltpu

F32 = jnp.float32
BF16 = jnp.bfloat16
I32 = jnp.int32

ML_HEADS = 4
ML_DK = 128
ML_DV = 128
DA_HEADS = 4
DA_QK = 64
DA_V = 128
ROPE_THETA = 10000.0
MOE_GROUPS = 4
MOE_PER_GROUP = 8
MOE_EXPERTS = MOE_GROUPS * MOE_PER_GROUP
MOE_TOP_K = 2
NORM_EPS = 1e-6
SUBLN_EPS = 1e-5
LAYER_INDEX = 0
LAM_INIT = 0.8 - 0.6 * math.exp(-0.3 * LAYER_INDEX)

LANES = 128
SUBLANES = 8
BF16_SUBLANES = 16
VMEM_LIMIT_BYTES = 56 * 1024 * 1024

TOKEN_TILE = 256
MLSTM_CHUNK = 256
ATTN_BLOCK = 512
PAGES_PER_STEP = 8
EXPERT_ROWS = 512
ROUTE_TILE = 512
ROUTER_ROWS = 48
SAMPLE_PAD = 16
NEW_KV_PAD = 128

NT_DIMS = (((1,), (1,)), ((), ()))
TN_DIMS = (((0,), (0,)), ((), ()))


def _params(*sem):
    return pltpu.CompilerParams(dimension_semantics=sem, vmem_limit_bytes=VMEM_LIMIT_BYTES)


def _sigmoid(x):
    return 1.0 / (1.0 + jnp.exp(-x))


def _log_sigmoid(x):
    return jnp.minimum(x, 0.0) - jnp.log1p(jnp.exp(-jnp.abs(x)))


def _rms(x, eps):
    return x * lax.rsqrt(jnp.mean(x * x, axis=-1, keepdims=True) + eps)


def _dot(a, b):
    return jnp.dot(a, b, preferred_element_type=F32)


def _dot_nt(a, b):
    return lax.dot_general(a, b, NT_DIMS, preferred_element_type=F32)


def _dot_tn(a, b):
    return lax.dot_general(a, b, TN_DIMS, preferred_element_type=F32)


def _split3(a):
    hi = a.astype(BF16)
    r1 = a - hi.astype(F32)
    mid = r1.astype(BF16)
    lo = (r1 - mid.astype(F32)).astype(BF16)
    return hi, mid, lo


def _inproj_kernel(xp_ref, xs_ref, g_ref, wm_ref, wkt_ref, wgr_ref, wgc_ref, br_ref, bc_ref, cos_ref, sin_ref,
                   cost_ref, sint_ref,
                   q_ref, k_ref, v_ref, og_ref, grow_ref, gcol_ref, qa_ref, vab_ref, sga_ref, sgb_ref,
                   ktp_ref, ktbp_ref, vp_ref, kts_ref, ktbs_ref, vs_ref, *, n_prompt_tiles):
    i = pl.program_id(0)
    is_prompt = i < n_prompt_tiles
    x = jnp.where(is_prompt, xp_ref[...], xs_ref[...])
    xn = (_rms(x, NORM_EPS) * g_ref[...]).astype(BF16)

    def mm(lo, hi):
        return _dot(xn, wm_ref[:, lo:hi])

    w = ML_HEADS * ML_DK
    q_ref[...] = mm(0, w).astype(BF16)
    k_ref[...] = (mm(w, 2 * w) * (ML_DK ** -0.5)).astype(BF16)
    v_ref[...] = mm(2 * w, 3 * w).astype(BF16)
    og_ref[...] = _sigmoid(mm(3 * w, 4 * w)).astype(BF16)

    gr = _dot_nt(wgr_ref[...], xn) + br_ref[...]
    rr = lax.broadcasted_iota(I32, gr.shape, 0)
    grow_ref[...] = jnp.where(rr >= ML_HEADS, _log_sigmoid(gr), gr)
    gc = _dot(xn, wgc_ref[...]) + bc_ref[...]
    cc = lax.broadcasted_iota(I32, gc.shape, 1)
    gcol_ref[...] = jnp.where(cc >= ML_HEADS, _log_sigmoid(gc), gc)

    aw = DA_HEADS * 2 * DA_QK
    reps = aw // LANES
    cosv = jnp.concatenate([cos_ref[...]] * reps, axis=1)
    sinv = jnp.concatenate([sin_ref[...]] * reps, axis=1)
    half = DA_QK // 2
    base = 4 * w
    zq = mm(base, base + aw)
    lane = lax.broadcasted_iota(I32, zq.shape, 1)
    partner = jnp.where((lane % DA_QK) < half, pltpu.roll(zq, aw - half, axis=1), pltpu.roll(zq, half, axis=1))
    qa_ref[...] = ((zq * cosv + partner * sinv) * (DA_QK ** -0.5)).astype(BF16)

    kgroups = aw // DA_QK
    zk = _dot_nt(wkt_ref[...], xn)
    pieces = []
    for g in range(kgroups):
        pieces += [zk[g * DA_QK + half:(g + 1) * DA_QK, :], zk[g * DA_QK:g * DA_QK + half, :]]
    zk_partner = jnp.concatenate(pieces, axis=0)
    kt = (zk * jnp.concatenate([cost_ref[...]] * kgroups, axis=0)
          + zk_partner * jnp.concatenate([sint_ref[...]] * kgroups, axis=0))

    va = mm(base + aw, base + 2 * aw)
    vab_ref[...] = va.astype(BF16)
    tm = va.shape[0]

    def store_kv(kt_out, ktb_out, v_out):
        kt_out[...] = kt
        ktb_out[...] = kt.astype(BF16)
        for h in range(DA_HEADS):
            v_out[pl.ds(h, tm, stride=DA_HEADS), :] = va[:, h * DA_V:(h + 1) * DA_V]

    @pl.when(is_prompt)
    def _():
        store_kv(ktp_ref, ktbp_ref, vp_ref)

    @pl.when(jnp.logical_not(is_prompt))
    def _():
        store_kv(kts_ref, ktbs_ref, vs_ref)

    base = base + 2 * aw
    d = g_ref.shape[-1]
    sga_ref[...] = _sigmoid(mm(base, base + d)).astype(BF16)
    sgb_ref[...] = _sigmoid(mm(base + d, base + 2 * d)).astype(BF16)


def _inproj(x_p, x_s, g_mix, w_main, w_kt, w_gr, w_gc, b_row, b_col, cos_t, sin_t, cos_tt, sin_tt, tm, seq):
    tp, d = x_p.shape
    ts = x_s.shape[0]
    npt, nst = tp // tm, ts // tm
    n_pos_tiles = seq // tm
    batch = tp // seq
    t_all = tp + ts
    w = ML_HEADS * ML_DK
    aw = DA_HEADS * 2 * DA_QK
    ncols = w_main.shape[1]

    def tok(i):
        return (i, 0)

    def const(i):
        return (0, 0)

    def p_idx(i):
        return (jnp.minimum(i, npt - 1), 0)

    def s_idx(i):
        return (jnp.maximum(i - npt, 0), 0)

    def pos_idx(i):
        return (jnp.where(i < npt, i % n_pos_tiles, n_pos_tiles), 0)

    def pos_idx_t(i):
        return (0, jnp.where(i < npt, i % n_pos_tiles, n_pos_tiles))

    def ktp_idx(i):
        ip = jnp.minimum(i, npt - 1)
        return (ip // n_pos_tiles, ip % n_pos_tiles)

    def kts_idx(i):
        return (0, jnp.maximum(i - npt, 0))

    bf = lambda n: jax.ShapeDtypeStruct((t_all, n), BF16)
    out_shape = (bf(w), bf(w), bf(w), bf(w),
                 jax.ShapeDtypeStruct((BF16_SUBLANES, t_all), F32),
                 jax.ShapeDtypeStruct((t_all, LANES), F32),
                 bf(aw), bf(aw), bf(d), bf(d),
                 jax.ShapeDtypeStruct((batch * aw, seq), F32), jax.ShapeDtypeStruct((batch * aw, seq), BF16),
                 jax.ShapeDtypeStruct((tp * DA_HEADS, DA_V), F32),
                 jax.ShapeDtypeStruct((aw, ts), F32), jax.ShapeDtypeStruct((aw, ts), BF16),
                 jax.ShapeDtypeStruct((ts * DA_HEADS, DA_V), F32))
    out_specs = (pl.BlockSpec((tm, w), tok), pl.BlockSpec((tm, w), tok), pl.BlockSpec((tm, w), tok),
                 pl.BlockSpec((tm, w), tok),
                 pl.BlockSpec((BF16_SUBLANES, tm), lambda i: (0, i)),
                 pl.BlockSpec((tm, LANES), tok),
                 pl.BlockSpec((tm, aw), tok), pl.BlockSpec((tm, aw), tok),
                 pl.BlockSpec((tm, d), tok), pl.BlockSpec((tm, d), tok),
                 pl.BlockSpec((aw, tm), ktp_idx), pl.BlockSpec((aw, tm), ktp_idx),
                 pl.BlockSpec((tm * DA_HEADS, DA_V), p_idx),
                 pl.BlockSpec((aw, tm), kts_idx), pl.BlockSpec((aw, tm), kts_idx),
                 pl.BlockSpec((tm * DA_HEADS, DA_V), s_idx))
    in_specs = [pl.BlockSpec((tm, d), p_idx), pl.BlockSpec((tm, d), s_idx),
                pl.BlockSpec((1, d), const),
                pl.BlockSpec((d, ncols), const),
                pl.BlockSpec((aw, d), const),
                pl.BlockSpec((BF16_SUBLANES, d), const),
                pl.BlockSpec((d, LANES), const),
                pl.BlockSpec((BF16_SUBLANES, 1), const),
                pl.BlockSpec((1, LANES), const),
                pl.BlockSpec((tm, LANES), pos_idx), pl.BlockSpec((tm, LANES), pos_idx),
                pl.BlockSpec((DA_QK, tm), pos_idx_t), pl.BlockSpec((DA_QK, tm), pos_idx_t)]
    return pl.pallas_call(
        functools.partial(_inproj_kernel, n_prompt_tiles=npt),
        grid=(npt + nst,), in_specs=in_specs, out_specs=out_specs, out_shape=out_shape,
        compiler_params=_params("arbitrary"), name="inproj",
    )(x_p, x_s, g_mix, w_main, w_kt, w_gr, w_gc, b_row, b_col, cos_t, sin_t, cos_tt, sin_tt)


def _mlstm_kernel(q_ref, k_ref, v_ref, og_ref, grow_ref, gcol_ref, c0_ref, n0_ref, m0_ref,
                  hg_ref, c_out, n_out, m_out, c_s, n_s, m_s, *, chunk):
    c = pl.program_id(1)
    nc = pl.num_programs(1)

    @pl.when(c == 0)
    def _():
        c_s[...] = c0_ref[...]
        n_s[...] = n0_ref[...]
        m_s[...] = m0_ref[...]

    L = chunk
    row = lax.broadcasted_iota(I32, (L, L), 0)
    col = lax.broadcasted_iota(I32, (L, L), 1)
    causal = col <= row
    tri = causal.astype(BF16)
    tri_t = (row <= col).astype(BF16)

    g_row = grow_ref[...]
    g_col = gcol_ref[...]
    cum_row = sum(_dot(p, tri_t) for p in _split3(g_row))
    cum_col = sum(_dot(tri, p) for p in _split3(g_col))

    for h in range(ML_HEADS):
        lo, hi = h * ML_DK, (h + 1) * ML_DK
        f = ML_HEADS + h
        b_col = cum_col[:, f:f + 1]
        ig_col = g_col[:, h:h + 1]
        b_row = cum_row[f:f + 1, :]
        ig_row = g_row[h:h + 1, :]
        m0 = m_s[h:h + 1, 0:1]
        c0 = c_s[h]
        n0 = n_s[h:h + 1, :]
        qh = q_ref[:, lo:hi]
        kh = k_ref[:, lo:hi]
        vh = v_ref[:, h * ML_DV:(h + 1) * ML_DV]

        dmat = jnp.where(causal, b_col - b_row + ig_row, -jnp.inf)
        inter = b_col + m0
        mt = jnp.maximum(inter, jnp.max(dmat, axis=-1, keepdims=True))
        wts = jnp.exp(dmat - mt) * _dot_nt(qh, kh)
        decay0 = jnp.exp(inter - mt)
        num = _dot(wts.astype(BF16), vh) + decay0 * _dot_nt(qh, c0.astype(BF16))
        qn = jnp.sum(qh.astype(F32) * n0, axis=-1, keepdims=True)
        den = jnp.sum(wts, axis=-1, keepdims=True) + decay0 * qn
        hh = num / jnp.maximum(jnp.abs(den), jnp.exp(-mt))
        hg_ref[:, h * ML_DV:(h + 1) * ML_DV] = (hh * og_ref[:, h * ML_DV:(h + 1) * ML_DV].astype(F32)).astype(BF16)

        b_last = b_col[L - 1:L, :]
        m_new = mt[L - 1:L, :]
        g_last = jnp.exp(b_last + m0 - m_new)
        ws = jnp.exp(b_last - b_col + ig_col - m_new)
        vw = (vh.astype(F32) * ws).astype(BF16)
        c_s[h] = g_last * c0 + _dot_tn(vw, kh)
        n_s[h:h + 1, :] = g_last * n0 + jnp.sum(kh.astype(F32) * ws, axis=0, keepdims=True)
        m_s[h:h + 1, :] = jnp.broadcast_to(m_new, (1, LANES))

    @pl.when(c == nc - 1)
    def _():
        c_out[...] = c_s[...]
        n_out[...] = n_s[...]
        m_out[...] = m_s[...]


def _mlstm(q, k, v, og, grow3, gcol, c0, n0, m0, *, batch, chunk, row_block_offset):
    nchunks_total = grow3.shape[0]
    nc = nchunks_total // batch
    w = ML_HEADS * ML_DK
    wv = ML_HEADS * ML_DV

    def tok(b, c):
        return (row_block_offset + b * nc + c, 0)

    def tok0(b, c):
        return (b * nc + c, 0)

    def st4(b, c):
        return (b, 0, 0, 0)

    def st3(b, c):
        return (b, 0, 0)

    rows = batch * nc * chunk
    out_shape = (jax.ShapeDtypeStruct((rows, wv), BF16),
                 jax.ShapeDtypeStruct(c0.shape, F32),
                 jax.ShapeDtypeStruct(n0.shape, F32),
                 jax.ShapeDtypeStruct(m0.shape, F32))
    in_specs = [pl.BlockSpec((chunk, w), tok), pl.BlockSpec((chunk, w), tok), pl.BlockSpec((chunk, wv), tok),
                pl.BlockSpec((chunk, wv), tok),
                pl.BlockSpec((None, BF16_SUBLANES, chunk), lambda b, c: (b * nc + c, 0, 0)),
                pl.BlockSpec((chunk, LANES), tok),
                pl.BlockSpec((None, ML_HEADS, ML_DV, ML_DK), st4),
                pl.BlockSpec((None, ML_HEADS, ML_DK), st3),
                pl.BlockSpec((None, ML_HEADS, LANES), st3)]
    out_specs = (pl.BlockSpec((chunk, wv), tok0),
                 pl.BlockSpec((None, ML_HEADS, ML_DV, ML_DK), st4),
                 pl.BlockSpec((None, ML_HEADS, ML_DK), st3),
                 pl.BlockSpec((None, ML_HEADS, LANES), st3))
    scratch = [pltpu.VMEM((ML_HEADS, ML_DV, ML_DK), F32), pltpu.VMEM((ML_HEADS, ML_DK), F32),
               pltpu.VMEM((ML_HEADS, LANES), F32)]
    return pl.pallas_call(
        functools.partial(_mlstm_kernel, chunk=chunk),
        grid=(batch, nc), in_specs=in_specs, out_specs=out_specs, out_shape=out_shape,
        scratch_shapes=scratch, compiler_params=_params("arbitrary", "arbitrary"), name="mlstm",
    )(q, k, v, og, grow3, gcol, c0, n0, m0)


def _lambda(lamv_ref):
    lv = lamv_ref[...]
    s1 = jnp.sum(lv[0:1, :] * lv[1:2, :], axis=-1, keepdims=True)
    s2 = jnp.sum(lv[2:3, :] * lv[3:4, :], axis=-1, keepdims=True)
    return jnp.exp(s1) - jnp.exp(s2) + LAM_INIT


def _subln(o, gsub_ref):
    return _rms(o, SUBLN_EPS) * gsub_ref[...] * (1.0 - LAM_INIT)


def _online_softmax_step(s, v, m_s, l_s, acc_s):
    m_prev = m_s[...]
    m_new = jnp.maximum(m_prev, jnp.max(s, axis=-1, keepdims=True))
    alpha = jnp.exp(m_prev - m_new)
    p = jnp.exp(s - m_new)
    l_s[...] = alpha * l_s[...] + jnp.sum(p, axis=-1, keepdims=True)
    acc_s[...] = alpha * acc_s[...] + _dot(p.astype(BF16), v)
    m_s[...] = m_new


def _attn_kernel(qt_ref, kt_ref, q_ref, k_ref, v_ref, lamv_ref, gsub_ref, o_ref, q2_s, m_s, l_s, acc_s, *, blk):
    p = pl.program_id(2)
    qi = qt_ref[p]
    ki = kt_ref[p]

    @pl.when(ki == 0)
    def _():
        q = q_ref[...]
        lane = lax.broadcasted_iota(I32, q.shape, 1)
        zero = jnp.zeros_like(q)
        q2_s[0:blk, :] = jnp.where(lane < DA_QK, q, zero)
        q2_s[blk:2 * blk, :] = jnp.where(lane >= DA_QK, q, zero)
        m_s[...] = jnp.full(m_s.shape, -jnp.inf, F32)
        l_s[...] = jnp.zeros(l_s.shape, F32)
        acc_s[...] = jnp.zeros(acc_s.shape, F32)

    s = _dot(q2_s[...], k_ref[...])

    @pl.when(ki < qi)
    def _():
        _online_softmax_step(s, v_ref[...], m_s, l_s, acc_s)

    @pl.when(ki == qi)
    def _():
        r = lax.broadcasted_iota(I32, s.shape, 0) % blk
        cidx = lax.broadcasted_iota(I32, s.shape, 1)
        _online_softmax_step(jnp.where(cidx <= r, s, -jnp.inf), v_ref[...], m_s, l_s, acc_s)
        o2 = acc_s[...] / l_s[...]
        o = o2[0:blk, :] - _lambda(lamv_ref) * o2[blk:2 * blk, :]
        o_ref[...] = _subln(o, gsub_ref).astype(BF16)


def _attn_prompt(qa, ktb, vab, lamv, gsub, *, batch, seq, blk):
    nq = seq // blk
    pairs = [(qi, ki) for qi in range(nq) for ki in range(qi + 1)]
    qt = jnp.asarray([p[0] for p in pairs], I32)
    kt = jnp.asarray([p[1] for p in pairs], I32)
    hw = 2 * DA_QK

    def q_idx(b, h, p, qt, kt):
        return (b * nq + qt[p], h)

    def k_idx(b, h, p, qt, kt):
        return (b * DA_HEADS + h, kt[p])

    def v_idx(b, h, p, qt, kt):
        return (b * nq + kt[p], h)

    def const(b, h, p, qt, kt):
        return (0, 0)

    grid_spec = pltpu.PrefetchScalarGridSpec(
        num_scalar_prefetch=2, grid=(batch, DA_HEADS, len(pairs)),
        in_specs=[pl.BlockSpec((blk, hw), q_idx), pl.BlockSpec((hw, blk), k_idx), pl.BlockSpec((blk, DA_V), v_idx),
                  pl.BlockSpec(lamv.shape, const), pl.BlockSpec((1, DA_V), const)],
        out_specs=pl.BlockSpec((blk, DA_V), q_idx),
        scratch_shapes=[pltpu.VMEM((2 * blk, hw), BF16), pltpu.VMEM((2 * blk, 1), F32),
                        pltpu.VMEM((2 * blk, 1), F32), pltpu.VMEM((2 * blk, DA_V), F32)])
    return pl.pallas_call(
        functools.partial(_attn_kernel, blk=blk), grid_spec=grid_spec,
        out_shape=jax.ShapeDtypeStruct((batch * seq, DA_HEADS * DA_V), BF16),
        compiler_params=_params("arbitrary", "arbitrary", "arbitrary"), name="attn_prompt",
    )(qt, kt, qa, ktb, vab, lamv, gsub)


def _dec_attn_kernel(pt_ref, q_ref, *refs, pages, n_new):
    k_refs = refs[:pages]
    v_refs = refs[pages:2 * pages]
    kn_ref, vn_ref, lamv_ref, gsub_ref, o_ref, m_s, l_s, acc_s = refs[2 * pages:]
    j = pl.program_id(1)
    nj = pl.num_programs(1)

    @pl.when(j == 0)
    def _():
        m_s[...] = jnp.full(m_s.shape, -jnp.inf, F32)
        l_s[...] = jnp.zeros(l_s.shape, F32)
        acc_s[...] = jnp.zeros(acc_s.shape, F32)

    page = k_refs[0].shape[1]
    half_rows = DA_HEADS * SUBLANES
    q = q_ref[...]

    def v_page(vr):
        return jnp.concatenate([vr[pl.ds(h, page, stride=DA_HEADS), :] for h in range(DA_HEADS)], axis=1).astype(BF16)

    s = jnp.concatenate([_dot(q, k_refs[p][...].astype(BF16)) for p in range(pages)], axis=1)
    m_prev = m_s[...]
    m_new = jnp.maximum(m_prev, jnp.max(s, axis=-1, keepdims=True))
    alpha = jnp.exp(m_prev - m_new)
    pr = jnp.exp(s - m_new)
    pv = sum(_dot(pr[:, p * page:(p + 1) * page].astype(BF16), v_page(v_refs[p])) for p in range(pages))
    l_s[...] = alpha * l_s[...] + jnp.sum(pr, axis=-1, keepdims=True)
    acc_s[...] = alpha * acc_s[...] + pv
    m_s[...] = m_new

    @pl.when(j == nj - 1)
    def _():
        sn = _dot(q, kn_ref[...])
        t = jnp.minimum(lax.broadcasted_iota(I32, sn.shape, 0) % SUBLANES, n_new - 1)
        cidx = lax.broadcasted_iota(I32, sn.shape, 1)
        _online_softmax_step(jnp.where(cidx <= t, sn, -jnp.inf), vn_ref[...], m_s, l_s, acc_s)
        o2 = acc_s[...] / l_s[...]
        lam = _lambda(lamv_ref)
        outs = []
        for h in range(DA_HEADS):
            r0 = h * SUBLANES
            o0 = o2[r0:r0 + SUBLANES, h * DA_V:(h + 1) * DA_V]
            o1 = o2[half_rows + r0:half_rows + r0 + SUBLANES, h * DA_V:(h + 1) * DA_V]
            outs.append(_subln(o0 - lam * o1, gsub_ref))
        o_ref[...] = jnp.concatenate(outs, axis=1)


def _attn_sample(page_table, qbd, cache_kt, cache_v2, knt, vn, lamv, gsub, *, n_new, page):
    bs, npg = page_table.shape
    pages = min(PAGES_PER_STEP, npg)
    while npg % pages:
        pages -= 1
    rows, width = qbd.shape[1], qbd.shape[2]
    vrows = page * DA_HEADS

    def k_spec(p):
        return pl.BlockSpec((width, page), lambda b, j, pt: (pt[b, j * pages + p], 0))

    def v_spec(p):
        return pl.BlockSpec((vrows, DA_V), lambda b, j, pt: (pt[b, j * pages + p], 0))

    def seq3(b, j, pt):
        return (b, 0, 0)

    def const(b, j, pt):
        return (0, 0)

    in_specs = ([pl.BlockSpec((None, rows, width), seq3)]
                + [k_spec(p) for p in range(pages)] + [v_spec(p) for p in range(pages)]
                + [pl.BlockSpec((None, width, NEW_KV_PAD), seq3), pl.BlockSpec((None, NEW_KV_PAD, width), seq3),
                   pl.BlockSpec(lamv.shape, const), pl.BlockSpec((1, DA_V), const)])
    grid_spec = pltpu.PrefetchScalarGridSpec(
        num_scalar_prefetch=1, grid=(bs, npg // pages), in_specs=in_specs,
        out_specs=pl.BlockSpec((None, SUBLANES, width), seq3),
        scratch_shapes=[pltpu.VMEM((rows, 1), F32), pltpu.VMEM((rows, 1), F32), pltpu.VMEM((rows, width), F32)])
    return pl.pallas_call(
        functools.partial(_dec_attn_kernel, pages=pages, n_new=n_new), grid_spec=grid_spec,
        out_shape=jax.ShapeDtypeStruct((bs, SUBLANES, width), F32),
        compiler_params=_params("arbitrary", "arbitrary"), name="attn_sample",
    )(page_table, qbd, *([cache_kt] * pages), *([cache_v2] * pages), knt, vn, lamv, gsub)


def _merge_kernel(xp_ref, xs_ref, hgp_ref, hgs_ref, op_ref, os_ref, sga_ref, sgb_ref, wa_ref, wb_ref, wo_ref,
                  gffn_ref, wr_ref, br_ref, x1_ref, lg_ref, *, n_prompt_tiles):
    is_prompt = pl.program_id(0) < n_prompt_tiles
    x = jnp.where(is_prompt, xp_ref[...], xs_ref[...])
    hg = jnp.where(is_prompt, hgp_ref[...], hgs_ref[...])
    o = jnp.where(is_prompt, op_ref[...], os_ref[...])
    mixed = sga_ref[...].astype(F32) * _dot(hg, wa_ref[...]) + sgb_ref[...].astype(F32) * _dot(o, wb_ref[...])
    x1 = x + _dot(mixed.astype(BF16), wo_ref[...])
    x1_ref[...] = x1
    xn = (_rms(x1, NORM_EPS) * gffn_ref[...]).astype(BF16)
    lg_ref[...] = _dot_nt(wr_ref[...], xn) + br_ref[...]


def _merge(x_p, x_s, hg_p, hg_s, o_p, o_s, sga, sgb, wa, wb, wo, g_ffn, wr, br, tm):
    tp, d = x_p.shape
    ts = x_s.shape[0]
    npt, nst = tp // tm, ts // tm
    t_all = tp + ts

    def tok(i):
        return (i, 0)

    def const(i):
        return (0, 0)

    def p_idx(i):
        return (jnp.minimum(i, npt - 1), 0)

    def s_idx(i):
        return (jnp.maximum(i - npt, 0), 0)

    wv, wo_in = hg_p.shape[1], o_p.shape[1]
    in_specs = [pl.BlockSpec((tm, d), p_idx), pl.BlockSpec((tm, d), s_idx),
                pl.BlockSpec((tm, wv), p_idx), pl.BlockSpec((tm, wv), s_idx),
                pl.BlockSpec((tm, wo_in), p_idx), pl.BlockSpec((tm, wo_in), s_idx),
                pl.BlockSpec((tm, d), tok), pl.BlockSpec((tm, d), tok),
                pl.BlockSpec(wa.shape, const), pl.BlockSpec(wb.shape, const), pl.BlockSpec(wo.shape, const),
                pl.BlockSpec((1, d), const), pl.BlockSpec(wr.shape, const), pl.BlockSpec((ROUTER_ROWS, 1), const)]
    out_shape = (jax.ShapeDtypeStruct((t_all, d), F32), jax.ShapeDtypeStruct((ROUTER_ROWS, t_all), F32))
    out_specs = (pl.BlockSpec((tm, d), tok), pl.BlockSpec((ROUTER_ROWS, tm), lambda i: (0, i)))
    return pl.pallas_call(
        functools.partial(_merge_kernel, n_prompt_tiles=npt),
        grid=(npt + nst,), in_specs=in_specs, out_specs=out_specs, out_shape=out_shape,
        compiler_params=_params("arbitrary"), name="merge",
    )(x_p, x_s, hg_p, hg_s, o_p, o_s, sga, sgb, wa, wb, wo, g_ffn, wr, br)


def _route_kernel(lg_ref, eid_ref, gw_ref):
    x = lg_ref[...]
    sub = lax.broadcasted_iota(I32, (SUBLANES, x.shape[1]), 0)
    lg = jnp.where(sub < MOE_GROUPS, x[0:SUBLANES, :], -jnp.inf)
    gmax = jnp.max(lg, axis=0, keepdims=True)
    g_star = jnp.min(jnp.where(lg == gmax, sub, SUBLANES), axis=0, keepdims=True)
    pg_top = 1.0 / jnp.sum(jnp.exp(lg - gmax), axis=0, keepdims=True)
    le = x[SUBLANES:2 * SUBLANES, :]
    for g in range(1, MOE_GROUPS):
        le = jnp.where(g_star == g, x[(g + 1) * SUBLANES:(g + 2) * SUBLANES, :], le)
    ex = jnp.exp(le - jnp.max(le, axis=0, keepdims=True))
    pe = ex / jnp.sum(ex, axis=0, keepdims=True)
    v1 = jnp.max(pe, axis=0, keepdims=True)
    i1 = jnp.min(jnp.where(pe == v1, sub, SUBLANES), axis=0, keepdims=True)
    rest = jnp.where(sub == i1, -jnp.inf, pe)
    v2 = jnp.max(rest, axis=0, keepdims=True)
    i2 = jnp.min(jnp.where(rest == v2, sub, SUBLANES), axis=0, keepdims=True)
    tot = v1 + v2
    e1 = g_star * MOE_PER_GROUP + i1
    e2 = g_star * MOE_PER_GROUP + i2
    w1 = pg_top * (v1 / tot)
    w2 = pg_top * (v2 / tot)
    eid_ref[...] = jnp.where(sub == 0, e1, jnp.where(sub == 1, e2, 0))
    gw_ref[...] = jnp.where(sub == 0, w1, jnp.where(sub == 1, w2, 0.0))


def _route(lg):
    rows, t_all = lg.shape
    tb = ROUTE_TILE
    while t_all % tb:
        tb //= 2
    spec = pl.BlockSpec((SUBLANES, tb), lambda i: (0, i))
    return pl.pallas_call(
        _route_kernel, grid=(t_all // tb,),
        in_specs=[pl.BlockSpec((rows, tb), lambda i: (0, i))], out_specs=(spec, spec),
        out_shape=(jax.ShapeDtypeStruct((SUBLANES, t_all), I32), jax.ShapeDtypeStruct((SUBLANES, t_all), F32)),
        compiler_params=_params("arbitrary"), name="route",
    )(lg)


def _expert_kernel(iblk_ref, iexp_ref, ilo_ref, ihi_ref, nit_ref, idx_hbm, x_hbm, w_ref, gffn_ref, wg_ref, wu_ref, wd_ref,
                   y_hbm, idx_s, xbuf, ybuf, wg_s, wu_s, wd_s, sem_i, sem_g, sem_s, *, rows, n_blocks, n_tokens):
    i = pl.program_id(0)
    valid = i < nit_ref[0]
    blk = iblk_ref[i]
    lo = ilo_ref[i]
    hi = ihi_ref[i]
    first = lo == 0
    last = hi == rows
    xs = blk % 2

    def idx_copy(b):
        return pltpu.make_async_copy(idx_hbm.at[b], idx_s.at[b % 3], sem_i.at[b % 3])

    def issue_gather(b):
        slot, s3 = b % 2, b % 3

        def body(r, carry):
            dst = idx_s[s3, r]
            tok = jnp.where(dst >= n_tokens, dst - n_tokens, dst)
            pltpu.make_async_copy(x_hbm.at[pl.ds(tok, 1)], xbuf.at[slot, pl.ds(r, 1)], sem_g.at[slot]).start()
            return carry

        lax.fori_loop(0, rows, body, 0)

    def wait_gather(slot):
        pltpu.make_async_copy(x_hbm.at[pl.ds(0, rows)], xbuf.at[slot], sem_g.at[slot]).wait()

    def issue_scatter(b):
        slot, s3 = b % 2, b % 3

        def body(r, carry):
            dst = idx_s[s3, r]
            pltpu.make_async_copy(ybuf.at[slot, pl.ds(r, 1)], y_hbm.at[pl.ds(dst, 1)], sem_s.at[slot]).start()
            return carry

        lax.fori_loop(0, rows, body, 0)

    def wait_scatter(slot):
        pltpu.make_async_copy(ybuf.at[slot], y_hbm.at[pl.ds(0, rows)], sem_s.at[slot]).wait()

    @pl.when(jnp.logical_and(valid, i == 0))
    def _():
        idx_copy(0).start()
        idx_copy(0).wait()
        if n_blocks > 1:
            idx_copy(1).start()
        issue_gather(0)

    @pl.when(jnp.logical_and(valid, first))
    def _():
        wait_gather(xs)

        @pl.when(blk + 1 < n_blocks)
        def _():
            idx_copy(blk + 1).wait()
            issue_gather(blk + 1)

        @pl.when(blk + 2 < n_blocks)
        def _():
            idx_copy(blk + 2).start()

        @pl.when(blk >= 2)
        def _():
            wait_scatter(xs)

    changed = jnp.logical_or(i == 0, iexp_ref[i] != iexp_ref[jnp.maximum(i - 1, 0)])

    @pl.when(jnp.logical_and(valid, changed))
    def _():
        wg_s[...] = wg_ref[...].astype(BF16)
        wu_s[...] = wu_ref[...].astype(BF16)
        wd_s[...] = wd_ref[...].astype(BF16)

    def compute():
        xn = (_rms(xbuf[xs], NORM_EPS) * gffn_ref[...]).astype(BF16)
        g = _dot(xn, wg_s[...])
        u = _dot(xn, wu_s[...])
        hmid = (g * _sigmoid(g) * u).astype(BF16)
        r = lax.broadcasted_iota(I32, (rows, 1), 0)
        wrow = jnp.where(jnp.logical_and(r >= lo, r < hi), w_ref[...], 0.0)
        return _dot(hmid, wd_s[...]) * wrow

    @pl.when(jnp.logical_and(valid, first))
    def _():
        ybuf[xs] = compute()

    @pl.when(jnp.logical_and(valid, jnp.logical_not(first)))
    def _():
        ybuf[xs] = ybuf[xs] + compute()

    @pl.when(jnp.logical_and(valid, last))
    def _():
        issue_scatter(blk)

    @pl.when(jnp.logical_and(valid, i == nit_ref[0] - 1))
    def _():
        wait_scatter(xs)

        @pl.when(blk >= 1)
        def _():
            wait_scatter(1 - xs)


def _experts(item_blk, item_exp, item_lo, item_hi, n_items, idx_rows, x1, w_rows, g_ffn, w_gate, w_up, w_down, rows):
    n_blocks = idx_rows.shape[0]
    t_all, d = x1.shape
    ff = w_gate.shape[-1]
    n_max = item_blk.shape[0]

    def blk_idx(i, ib, ie, il, ih, nt):
        return (ib[i], 0)

    def const(i, ib, ie, il, ih, nt):
        return (0, 0)

    def w_idx(i, ib, ie, il, ih, nt):
        return (ie[i], 0, 0)

    grid_spec = pltpu.PrefetchScalarGridSpec(
        num_scalar_prefetch=5, grid=(n_max,),
        in_specs=[pl.BlockSpec(memory_space=pl.ANY), pl.BlockSpec(memory_space=pl.ANY),
                  pl.BlockSpec((rows, 1), blk_idx), pl.BlockSpec((1, d), const),
                  pl.BlockSpec((None, d, ff), w_idx), pl.BlockSpec((None, d, ff), w_idx),
                  pl.BlockSpec((None, ff, d), w_idx)],
        out_specs=pl.BlockSpec(memory_space=pl.ANY),
        scratch_shapes=[pltpu.SMEM((3, rows), I32), pltpu.VMEM((2, rows, d), F32), pltpu.VMEM((2, rows, d), F32),
                        pltpu.VMEM((d, ff), BF16), pltpu.VMEM((d, ff), BF16), pltpu.VMEM((ff, d), BF16),
                        pltpu.SemaphoreType.DMA((3,)), pltpu.SemaphoreType.DMA((2,)), pltpu.SemaphoreType.DMA((2,))])
    return pl.pallas_call(
        functools.partial(_expert_kernel, rows=rows, n_blocks=n_blocks, n_tokens=t_all), grid_spec=grid_spec,
        out_shape=jax.ShapeDtypeStruct((MOE_TOP_K * t_all, d), F32),
        compiler_params=_params("arbitrary"), name="experts",
    )(item_blk, item_exp, item_lo, item_hi, n_items, idx_rows, x1, w_rows, g_ffn, w_gate, w_up, w_down)


def _final_kernel(x1_ref, y0_ref, y1_ref, plep_ref, ples_ref, gple_ref, wpg_ref, wpp_ref, gfin_ref, yp_ref, ys_ref,
                  *, n_prompt_tiles):
    is_prompt = pl.program_id(0) < n_prompt_tiles
    x2 = x1_ref[...] + (y0_ref[...] + y1_ref[...])
    xn = (_rms(x2, NORM_EPS) * gple_ref[...]).astype(BF16)
    ple = jnp.where(is_prompt, plep_ref[...], ples_ref[...]).astype(BF16)
    x3 = x2 + _sigmoid(_dot(xn, wpg_ref[...])) * _dot(ple, wpp_ref[...])
    y = _rms(x3, NORM_EPS) * gfin_ref[...]

    @pl.when(is_prompt)
    def _():
        yp_ref[...] = y

    @pl.when(jnp.logical_not(is_prompt))
    def _():
        ys_ref[...] = y


def _final(x1, y_slots, ple_p, ple_s, g_ple, wpg, wpp, g_final, tm):
    t_all, d = x1.shape
    tp, ts = ple_p.shape[0], ple_s.shape[0]
    npt, nst = tp // tm, ts // tm
    pd = ple_p.shape[1]
    nt = npt + nst

    def tok(i):
        return (i, 0)

    def tok1(i):
        return (nt + i, 0)

    def const(i):
        return (0, 0)

    def p_idx(i):
        return (jnp.minimum(i, npt - 1), 0)

    def s_idx(i):
        return (jnp.maximum(i - npt, 0), 0)

    in_specs = [pl.BlockSpec((tm, d), tok), pl.BlockSpec((tm, d), tok), pl.BlockSpec((tm, d), tok1),
                pl.BlockSpec((tm, pd), p_idx), pl.BlockSpec((tm, pd), s_idx),
                pl.BlockSpec((1, d), const), pl.BlockSpec(wpg.shape, const), pl.BlockSpec(wpp.shape, const),
                pl.BlockSpec((1, d), const)]
    return pl.pallas_call(
        functools.partial(_final_kernel, n_prompt_tiles=npt),
        grid=(nt,), in_specs=in_specs,
        out_specs=(pl.BlockSpec((tm, d), p_idx), pl.BlockSpec((tm, d), s_idx)),
        out_shape=(jax.ShapeDtypeStruct((tp, d), F32), jax.ShapeDtypeStruct((ts, d), F32)),
        compiler_params=_params("arbitrary"), name="final",
    )(x1, y_slots, y_slots, ple_p, ple_s, g_ple, wpg, wpp, g_final)


def _rope_tables(pos):
    half = DA_QK // 2
    inv = ROPE_THETA ** (-jnp.arange(half, dtype=F32) / half)
    ang = pos.astype(F32)[:, None] * inv[None, :]
    cos, sin = jnp.cos(ang), jnp.sin(ang)
    reps = LANES // DA_QK
    cos_t = jnp.tile(jnp.concatenate([cos, cos], axis=1), (1, reps))
    sin_t = jnp.tile(jnp.concatenate([-sin, sin], axis=1), (1, reps))
    return cos_t, sin_t


def _tile(limit, *sizes):
    t = limit
    while any(s % t for s in sizes):
        t //= 2
    return t


def kernel(x_prompt, x_sample, cache_k, cache_v, state_mlstm_C, state_mlstm_n, state_mlstm_m, page_table, p_prompt, p_sample, g_mix, w_in, b_ml_i, b_ml_f, lam_q1, lam_k1, lam_q2, lam_k2, g_sub, w_br_a, w_br_b, w_out, g_ffn, w_rg, b_rg, w_re, b_re, w_e_gate, w_e_up, w_e_down, g_ple, w_ple_gate, w_ple_proj, g_final):
    depth = w_in.shape[0]
    assert depth == 1, "single-layer step"
    bp, sp, d = x_prompt.shape
    bs, ss, _ = x_sample.shape
    assert ss <= SUBLANES
    tp, ts = bp * sp, bs * ss
    t_all = tp + ts
    n_pages, page = page_table.shape[1], cache_k.shape[2]
    past_len = n_pages * page
    w = ML_HEADS * ML_DK
    aw = DA_HEADS * 2 * DA_QK
    li = 0

    wi = w_in[li]
    sizes = (w, w, ML_HEADS * ML_DV, ML_HEADS * ML_DV, ML_HEADS, ML_HEADS, aw, aw, DA_HEADS * DA_V, d, d)
    edges = [0]
    for n in sizes:
        edges.append(edges[-1] + n)
    assert edges[-1] == wi.shape[1]
    seg = [wi[:, edges[i]:edges[i + 1]] for i in range(11)]
    w_main = jnp.concatenate(seg[0:4] + [seg[6]] + seg[8:11], axis=1).astype(BF16)
    w_kt = seg[7].T.astype(BF16)
    w_gates = jnp.concatenate([seg[4], seg[5]], axis=1)
    w_gr = jnp.pad(w_gates.T, ((0, BF16_SUBLANES - 2 * ML_HEADS), (0, 0))).astype(BF16)
    w_gc = jnp.pad(w_gates, ((0, 0), (0, LANES - 2 * ML_HEADS))).astype(BF16)
    b_gates = jnp.concatenate([b_ml_i[li], b_ml_f[li]]).astype(F32)
    b_row = jnp.pad(b_gates, (0, BF16_SUBLANES - 2 * ML_HEADS))[:, None]
    b_col = jnp.pad(b_gates, (0, LANES - 2 * ML_HEADS))[None, :]
    lamv = jnp.stack([lam_q1[li], lam_k1[li], lam_q2[li], lam_k2[li]]).astype(F32)
    gsub = g_sub[li][None, :].astype(F32)
    w_router = jnp.zeros((ROUTER_ROWS, d), F32).at[0:MOE_GROUPS].set(w_rg[li].T).at[SUBLANES:SUBLANES + MOE_EXPERTS].set(w_re[li].T)
    b_router = jnp.zeros((ROUTER_ROWS,), F32).at[0:MOE_GROUPS].set(b_rg[li]).at[SUBLANES:SUBLANES + MOE_EXPERTS].set(b_re[li])

    tm = _tile(TOKEN_TILE, sp, ts)
    cos_p, sin_p = _rope_tables(jnp.arange(sp))
    cos_s, sin_s = _rope_tables(past_len + (jnp.arange(tm) % ss))
    cos_t = jnp.concatenate([cos_p, cos_s], axis=0)
    sin_t = jnp.concatenate([sin_p, sin_s], axis=0)
    cos_tt = cos_t[:, :DA_QK].T
    sin_tt = sin_t[:, :DA_QK].T
    xp2 = x_prompt.reshape(tp, d)
    xs2 = x_sample.reshape(ts, d)
    (q_ml, k_ml, v_ml, og, grow, gcol, qa, vab, sga, sgb, kt_p, ktb_p, v_p, kt_s, ktb_s, v_s) = _inproj(
        xp2, xs2, g_mix[li][None, :], w_main, w_kt, w_gr, w_gc, b_row, b_col, cos_t, sin_t, cos_tt, sin_tt, tm, sp)

    chunk = _tile(MLSTM_CHUNK, sp)
    ncp = sp // chunk
    grow3_p = grow[:, :tp].reshape(BF16_SUBLANES, bp * ncp, chunk).transpose(1, 0, 2)
    zc = jnp.zeros((bp, ML_HEADS, ML_DV, ML_DK), F32)
    zn = jnp.zeros((bp, ML_HEADS, ML_DK), F32)
    zm = jnp.zeros((bp, ML_HEADS, LANES), F32)
    hg_p, c_p, n_p, m_p = _mlstm(q_ml, k_ml, v_ml, og, grow3_p, gcol, zc, zn, zm,
                                 batch=bp, chunk=chunk, row_block_offset=0)

    padn = SAMPLE_PAD - ss

    def pad_seq(a):
        return jnp.pad(a[tp:].reshape(bs, ss, -1), ((0, 0), (0, padn), (0, 0))).reshape(bs * SAMPLE_PAD, -1)

    neutral = jnp.where(jnp.arange(LANES) < ML_HEADS, -1e30, 0.0).astype(F32)
    gcol_s = jnp.concatenate([gcol[tp:].reshape(bs, ss, LANES),
                              jnp.broadcast_to(neutral, (bs, padn, LANES))], axis=1).reshape(bs * SAMPLE_PAD, LANES)
    grow3_s = jnp.concatenate([grow[:, tp:].reshape(BF16_SUBLANES, bs, ss).transpose(1, 0, 2),
                               jnp.broadcast_to(neutral[:BF16_SUBLANES, None], (bs, BF16_SUBLANES, padn))], axis=2)
    m0_s = jnp.broadcast_to(state_mlstm_m[li].astype(F32)[:, :, None], (bs, ML_HEADS, LANES))
    hg_s_pad, c_s, n_s, m_s = _mlstm(pad_seq(q_ml), pad_seq(k_ml), pad_seq(v_ml), pad_seq(og), grow3_s, gcol_s,
                                     state_mlstm_C[li].astype(F32), state_mlstm_n[li].astype(F32), m0_s,
                                     batch=bs, chunk=SAMPLE_PAD, row_block_offset=0)
    hg_s = hg_s_pad.reshape(bs, SAMPLE_PAD, -1)[:, :ss].reshape(ts, -1)

    blk = _tile(ATTN_BLOCK, sp)
    o_p = _attn_prompt(qa, ktb_p, vab, lamv, gsub, batch=bp, seq=sp, blk=blk)

    q_s = jnp.pad(qa[tp:].reshape(bs, ss, DA_HEADS, 2, DA_QK), ((0, 0), (0, SUBLANES - ss), (0, 0), (0, 0), (0, 0)))
    q_cht = q_s.transpose(0, 3, 2, 1, 4)
    same = jnp.logical_and(
        (jnp.arange(DA_HEADS)[:, None] == jnp.arange(DA_HEADS)[None, :])[None, None, :, None, :, None, None],
        (jnp.arange(2)[:, None] == jnp.arange(2)[None, :])[None, :, None, None, None, :, None])
    qbd = jnp.where(same, q_cht[:, :, :, :, None, None, :], jnp.zeros((), BF16)).reshape(bs, 2 * DA_HEADS * SUBLANES, aw)
    knt = jnp.pad(ktb_s.reshape(aw, bs, ss).transpose(1, 0, 2), ((0, 0), (0, 0), (0, NEW_KV_PAD - ss)))
    vn = jnp.pad(vab[tp:].reshape(bs, ss, aw), ((0, 0), (0, NEW_KV_PAD - ss), (0, 0)))
    ckt = cache_k[li].transpose(0, 2, 3, 4, 1).reshape(-1, page)
    cv2 = cache_v[li].reshape(-1, DA_V)
    o_s = _attn_sample(page_table, qbd, ckt, cv2, knt, vn, lamv, gsub, n_new=ss, page=page)
    o_s = o_s[:, :ss].reshape(ts, aw).astype(BF16)

    x1, lg = _merge(xp2, xs2, hg_p, hg_s, o_p, o_s, sga, sgb,
                    w_br_a[li].astype(BF16), w_br_b[li].astype(BF16), w_out[li].astype(BF16),
                    g_ffn[li][None, :], w_router.astype(BF16), b_router[:, None], tm)

    eid8, gw8 = _route(lg)
    n_slots = t_all * MOE_TOP_K
    rows = _tile(EXPERT_ROWS, n_slots)
    nb = n_slots // rows
    flat_e = eid8[:MOE_TOP_K].T.reshape(-1)
    flat_w = gw8[:MOE_TOP_K].T.reshape(-1)
    slot_id = jnp.arange(n_slots, dtype=I32)
    sorted_e, order, sorted_w = lax.sort((flat_e, slot_id, flat_w), num_keys=1, is_stable=True)
    idx_rows = ((order % MOE_TOP_K) * t_all + order // MOE_TOP_K).reshape(nb, rows)
    counts = jnp.sum(flat_e[None, :] == jnp.arange(MOE_EXPERTS, dtype=I32)[:, None], axis=1, dtype=I32)
    ends = jnp.cumsum(counts)
    starts = ends - counts
    first_blk = starts // rows
    n_e = jnp.where(counts > 0, (ends - 1) // rows - first_blk + 1, 0)
    item_end = jnp.cumsum(n_e)
    item_start = item_end - n_e
    n_items = item_end[-1:]
    n_max = nb + MOE_EXPERTS - 1
    it = jnp.minimum(jnp.arange(n_max, dtype=I32), n_items[0] - 1)
    item_exp = jnp.sum(item_end[None, :] <= it[:, None], axis=1, dtype=I32)
    item_blk = first_blk[item_exp] + it - item_start[item_exp]
    item_lo = jnp.maximum(starts[item_exp] - item_blk * rows, 0)
    item_hi = jnp.minimum(ends[item_exp] - item_blk * rows, rows)
    y_slots = _experts(item_blk, item_exp, item_lo, item_hi, n_items, idx_rows, x1, sorted_w[:, None],
                       g_ffn[li][None, :], w_e_gate[li], w_e_up[li], w_e_down[li], rows)

    pd = p_prompt.shape[-1]
    y_p, y_s = _final(x1, y_slots, p_prompt[li].reshape(tp, pd), p_sample[li].reshape(ts, pd), g_ple[li][None, :],
                      w_ple_gate[li].astype(BF16), w_ple_proj[li].astype(BF16), g_final[None, :], tm)

    return (y_p.reshape(bp, sp, d), y_s.reshape(bs, ss, d),
            kt_p.reshape(1, bp, DA_HEADS, 2, DA_QK, sp).transpose(0, 1, 5, 2, 3, 4), v_p.reshape(1, bp, sp, DA_HEADS, DA_V),
            c_p[None], n_p[None], m_p[None, :, :, 0],
            kt_s.reshape(1, DA_HEADS, 2, DA_QK, bs, ss).transpose(0, 4, 5, 1, 2, 3), v_s.reshape(1, bs, ss, DA_HEADS, DA_V),
            c_s[None], n_s[None], m_s[None, :, :, 0])
```

```python
import functools
import math

import jax
import jax.numpy as jnp
from jax import lax
from jax.experimental import pallas as pl
from jax.experimental.pallas import tpu as pltpu

F32 = jnp.float32
BF16 = jnp.bfloat16
I32 = jnp.int32

ML_HEADS = 4
ML_DK = 128
ML_DV = 128
DA_HEADS = 4
DA_QK = 64
DA_V = 128
ROPE_THETA = 10000.0
MOE_GROUPS = 4
MOE_PER_GROUP = 8
MOE_EXPERTS = MOE_GROUPS * MOE_PER_GROUP
MOE_TOP_K = 2
NORM_EPS = 1e-6
SUBLN_EPS = 1e-5
LAYER_INDEX = 0
LAM_INIT = 0.8 - 0.6 * math.exp(-0.3 * LAYER_INDEX)

LANES = 128
SUBLANES = 8
BF16_SUBLANES = 16
VMEM_LIMIT_BYTES = 56 * 1024 * 1024

TOKEN_TILE = 512
MLSTM_CHUNK = 256
ATTN_BLOCK = 512
PAGES_PER_STEP = 8
EXPERT_ROWS = 512
ROUTE_TILE = 512
ROUTER_ROWS = 48
SAMPLE_PAD = 16
NEW_KV_PAD = 128
DMA_ISSUE_UNROLL = 8

NT_DIMS = (((1,), (1,)), ((), ()))
TN_DIMS = (((0,), (0,)), ((), ()))


def _params(*sem):
    return pltpu.CompilerParams(dimension_semantics=sem, vmem_limit_bytes=VMEM_LIMIT_BYTES)


def _sigmoid(x):
    return 1.0 / (1.0 + jnp.exp(-x))


def _log_sigmoid(x):
    return jnp.minimum(x, 0.0) - jnp.log1p(jnp.exp(-jnp.abs(x)))


def _rms(x, eps):
    return x * lax.rsqrt(jnp.mean(x * x, axis=-1, keepdims=True) + eps)


def _dot(a, b):
    return jnp.dot(a, b, preferred_element_type=F32)


def _dot_nt(a, b):
    return lax.dot_general(a, b, NT_DIMS, preferred_element_type=F32)


def _dot_tn(a, b):
    return lax.dot_general(a, b, TN_DIMS, preferred_element_type=F32)


def _slab_load(ref, rows, chunks, lead=None):
    def piece(c):
        idx = (pl.ds(c, rows, stride=chunks), slice(None))
        return ref[idx] if lead is None else ref[(lead,) + idx]
    return jnp.concatenate([piece(c) for c in range(chunks)], axis=1)


def _slab_store(ref, val, chunks, lead=None):
    rows = val.shape[0]
    for c in range(chunks):
        idx = (pl.ds(c, rows, stride=chunks), slice(None))
        ref[idx if lead is None else (lead,) + idx] = val[:, c * LANES:(c + 1) * LANES]


def _split3(a):
    hi = a.astype(BF16)
    r1 = a - hi.astype(F32)
    mid = r1.astype(BF16)
    lo = (r1 - mid.astype(F32)).astype(BF16)
    return hi, mid, lo


def _inproj_kernel(xp_ref, xs_ref, g_ref, wm_ref, wkt_ref, wgr_ref, wgc_ref, br_ref, bc_ref, cos_ref, sin_ref,
                   cost_ref, sint_ref,
                   q_ref, k_ref, v_ref, og_ref, grow_ref, gcol_ref, qa_ref, vab_ref, sga_ref, sgb_ref,
                   ktp_ref, ktbp_ref, vp_ref, kts_ref, ktbs_ref, vs_ref, *, n_prompt_tiles):
    i = pl.program_id(0)
    is_prompt = i < n_prompt_tiles
    x = jnp.where(is_prompt, xp_ref[...], xs_ref[...])
    xn = (_rms(x, NORM_EPS) * g_ref[...]).astype(BF16)

    def mm(lo, hi):
        return _dot(xn, wm_ref[:, lo:hi])

    w = ML_HEADS * ML_DK
    q_ref[...] = mm(0, w).astype(BF16)
    k_ref[...] = (mm(w, 2 * w) * (ML_DK ** -0.5)).astype(BF16)
    v_ref[...] = mm(2 * w, 3 * w).astype(BF16)
    og_ref[...] = _sigmoid(mm(3 * w, 4 * w)).astype(BF16)

    gr = _dot_nt(wgr_ref[...], xn) + br_ref[...]
    rr = lax.broadcasted_iota(I32, gr.shape, 0)
    grow_ref[...] = jnp.where(rr >= ML_HEADS, _log_sigmoid(gr), gr)
    gc = _dot(xn, wgc_ref[...]) + bc_ref[...]
    cc = lax.broadcasted_iota(I32, gc.shape, 1)
    gcol_ref[...] = jnp.where(cc >= ML_HEADS, _log_sigmoid(gc), gc)

    aw = DA_HEADS * 2 * DA_QK
    reps = aw // LANES
    cosv = jnp.concatenate([cos_ref[...]] * reps, axis=1)
    sinv = jnp.concatenate([sin_ref[...]] * reps, axis=1)
    half = DA_QK // 2
    base = 4 * w
    zq = mm(base, base + aw)
    lane = lax.broadcasted_iota(I32, zq.shape, 1)
    partner = jnp.where((lane % DA_QK) < half, pltpu.roll(zq, aw - half, axis=1), pltpu.roll(zq, half, axis=1))
    qa_ref[...] = ((zq * cosv + partner * sinv) * (DA_QK ** -0.5)).astype(BF16)

    kgroups = aw // DA_QK
    zk = _dot_nt(wkt_ref[...], xn)
    pieces = []
    for g in range(kgroups):
        pieces += [zk[g * DA_QK + half:(g + 1) * DA_QK, :], zk[g * DA_QK:g * DA_QK + half, :]]
    zk_partner = jnp.concatenate(pieces, axis=0)
    kt = (zk * jnp.concatenate([cost_ref[...]] * kgroups, axis=0)
          + zk_partner * jnp.concatenate([sint_ref[...]] * kgroups, axis=0))

    va = mm(base + aw, base + 2 * aw)
    vab_ref[...] = va.astype(BF16)
    tm = va.shape[0]

    def store_kv(kt_out, ktb_out, v_out):
        kt_out[...] = kt
        ktb_out[...] = kt.astype(BF16)
        for h in range(DA_HEADS):
            v_out[pl.ds(h, tm, stride=DA_HEADS), :] = va[:, h * DA_V:(h + 1) * DA_V]

    @pl.when(is_prompt)
    def _():
        store_kv(ktp_ref, ktbp_ref, vp_ref)

    @pl.when(jnp.logical_not(is_prompt))
    def _():
        store_kv(kts_ref, ktbs_ref, vs_ref)

    base = base + 2 * aw
    d = g_ref.shape[-1]
    sga_ref[...] = _sigmoid(mm(base, base + d)).astype(BF16)
    sgb_ref[...] = _sigmoid(mm(base + d, base + 2 * d)).astype(BF16)


def _inproj(x_p, x_s, g_mix, w_main, w_kt, w_gr, w_gc, b_row, b_col, cos_t, sin_t, cos_tt, sin_tt, tm, seq):
    tp, d = x_p.shape
    ts = x_s.shape[0]
    npt, nst = tp // tm, ts // tm
    n_pos_tiles = seq // tm
    batch = tp // seq
    t_all = tp + ts
    w = ML_HEADS * ML_DK
    aw = DA_HEADS * 2 * DA_QK
    ncols = w_main.shape[1]

    def tok(i):
        return (i, 0)

    def const(i):
        return (0, 0)

    def p_idx(i):
        return (jnp.minimum(i, npt - 1), 0)

    def s_idx(i):
        return (jnp.maximum(i - npt, 0), 0)

    def pos_idx(i):
        return (jnp.where(i < npt, i % n_pos_tiles, n_pos_tiles), 0)

    def pos_idx_t(i):
        return (0, jnp.where(i < npt, i % n_pos_tiles, n_pos_tiles))

    def ktp_idx(i):
        ip = jnp.minimum(i, npt - 1)
        return (ip // n_pos_tiles, ip % n_pos_tiles)

    def kts_idx(i):
        return (0, jnp.maximum(i - npt, 0))

    bf = lambda n: jax.ShapeDtypeStruct((t_all, n), BF16)
    out_shape = (bf(w), bf(w), bf(w), bf(w),
                 jax.ShapeDtypeStruct((BF16_SUBLANES, t_all), F32),
                 jax.ShapeDtypeStruct((t_all, LANES), F32),
                 bf(aw), bf(aw), bf(d), bf(d),
                 jax.ShapeDtypeStruct((batch * aw, seq), F32), jax.ShapeDtypeStruct((batch * aw, seq), BF16),
                 jax.ShapeDtypeStruct((tp * DA_HEADS, DA_V), F32),
                 jax.ShapeDtypeStruct((aw, ts), F32), jax.ShapeDtypeStruct((aw, ts), BF16),
                 jax.ShapeDtypeStruct((ts * DA_HEADS, DA_V), F32))
    out_specs = (pl.BlockSpec((tm, w), tok), pl.BlockSpec((tm, w), tok), pl.BlockSpec((tm, w), tok),
                 pl.BlockSpec((tm, w), tok),
                 pl.BlockSpec((BF16_SUBLANES, tm), lambda i: (0, i)),
                 pl.BlockSpec((tm, LANES), tok),
                 pl.BlockSpec((tm, aw), tok), pl.BlockSpec((tm, aw), tok),
                 pl.BlockSpec((tm, d), tok), pl.BlockSpec((tm, d), tok),
                 pl.BlockSpec((aw, tm), ktp_idx), pl.BlockSpec((aw, tm), ktp_idx),
                 pl.BlockSpec((tm * DA_HEADS, DA_V), p_idx),
                 pl.BlockSpec((aw, tm), kts_idx), pl.BlockSpec((aw, tm), kts_idx),
                 pl.BlockSpec((tm * DA_HEADS, DA_V), s_idx))
    in_specs = [pl.BlockSpec((tm, d), p_idx), pl.BlockSpec((tm, d), s_idx),
                pl.BlockSpec((1, d), const),
                pl.BlockSpec((d, ncols), const, pipeline_mode=pl.Buffered(1)),
                pl.BlockSpec((aw, d), const, pipeline_mode=pl.Buffered(1)),
                pl.BlockSpec((BF16_SUBLANES, d), const),
                pl.BlockSpec((d, LANES), const),
                pl.BlockSpec((BF16_SUBLANES, 1), const),
                pl.BlockSpec((1, LANES), const),
                pl.BlockSpec((tm, LANES), pos_idx), pl.BlockSpec((tm, LANES), pos_idx),
                pl.BlockSpec((DA_QK, tm), pos_idx_t), pl.BlockSpec((DA_QK, tm), pos_idx_t)]
    return pl.pallas_call(
        functools.partial(_inproj_kernel, n_prompt_tiles=npt),
        grid=(npt + nst,), in_specs=in_specs, out_specs=out_specs, out_shape=out_shape,
        compiler_params=_params("arbitrary"), name="inproj",
    )(x_p, x_s, g_mix, w_main, w_kt, w_gr, w_gc, b_row, b_col, cos_t, sin_t, cos_tt, sin_tt)


def _mlstm_kernel(q_ref, k_ref, v_ref, og_ref, grow_ref, gcol_ref, c0_ref, n0_ref, m0_ref,
                  hg_ref, c_out, n_out, m_out, c_s, n_s, m_s, *, chunk):
    c = pl.program_id(1)
    nc = pl.num_programs(1)

    @pl.when(c == 0)
    def _():
        c_s[...] = c0_ref[...]
        n_s[...] = n0_ref[...]
        m_s[...] = m0_ref[...]

    L = chunk
    row = lax.broadcasted_iota(I32, (L, L), 0)
    col = lax.broadcasted_iota(I32, (L, L), 1)
    causal = col <= row
    tri = causal.astype(BF16)
    tri_t = (row <= col).astype(BF16)

    g_row = grow_ref[...]
    g_col = gcol_ref[...]
    cum_row = sum(_dot(p, tri_t) for p in _split3(g_row))
    cum_col = sum(_dot(tri, p) for p in _split3(g_col))

    for h in range(ML_HEADS):
        lo, hi = h * ML_DK, (h + 1) * ML_DK
        f = ML_HEADS + h
        b_col = cum_col[:, f:f + 1]
        ig_col = g_col[:, h:h + 1]
        b_row = cum_row[f:f + 1, :]
        ig_row = g_row[h:h + 1, :]
        m0 = m_s[h:h + 1, 0:1]
        c0 = c_s[h]
        n0 = n_s[h:h + 1, :]
        qh = q_ref[:, lo:hi]
        kh = k_ref[:, lo:hi]
        vh = v_ref[:, h * ML_DV:(h + 1) * ML_DV]

        dmat = jnp.where(causal, b_col - b_row + ig_row, -jnp.inf)
        inter = b_col + m0
        mt = jnp.maximum(inter, jnp.max(dmat, axis=-1, keepdims=True))
        wts = jnp.exp(dmat - mt) * _dot_nt(qh, kh)
        decay0 = jnp.exp(inter - mt)
        num = _dot(wts.astype(BF16), vh) + decay0 * _dot_nt(qh, c0.astype(BF16))
        qn = jnp.sum(qh.astype(F32) * n0, axis=-1, keepdims=True)
        den = jnp.sum(wts, axis=-1, keepdims=True) + decay0 * qn
        hh = num / jnp.maximum(jnp.abs(den), jnp.exp(-mt))
        hg_ref[:, h * ML_DV:(h + 1) * ML_DV] = (hh * og_ref[:, h * ML_DV:(h + 1) * ML_DV].astype(F32)).astype(BF16)

        b_last = b_col[L - 1:L, :]
        m_new = mt[L - 1:L, :]
        g_last = jnp.exp(b_last + m0 - m_new)
        ws = jnp.exp(b_last - b_col + ig_col - m_new)
        vw = (vh.astype(F32) * ws).astype(BF16)
        c_s[h] = g_last * c0 + _dot_tn(vw, kh)
        n_s[h:h + 1, :] = g_last * n0 + jnp.sum(kh.astype(F32) * ws, axis=0, keepdims=True)
        m_s[h:h + 1, :] = jnp.broadcast_to(m_new, (1, LANES))

    @pl.when(c == nc - 1)
    def _():
        c_out[...] = c_s[...]
        n_out[...] = n_s[...]
        m_out[...] = m_s[...]


def _mlstm(q, k, v, og, grow3, gcol, c0, n0, m0, *, batch, chunk, row_block_offset):
    nchunks_total = grow3.shape[0]
    nc = nchunks_total // batch
    w = ML_HEADS * ML_DK
    wv = ML_HEADS * ML_DV

    def tok(b, c):
        return (row_block_offset + b * nc + c, 0)

    def tok0(b, c):
        return (b * nc + c, 0)

    def st4(b, c):
        return (b, 0, 0, 0)

    def st3(b, c):
        return (b, 0, 0)

    rows = batch * nc * chunk
    out_shape = (jax.ShapeDtypeStruct((rows, wv), BF16),
                 jax.ShapeDtypeStruct(c0.shape, F32),
                 jax.ShapeDtypeStruct(n0.shape, F32),
                 jax.ShapeDtypeStruct(m0.shape, F32))
    in_specs = [pl.BlockSpec((chunk, w), tok), pl.BlockSpec((chunk, w), tok), pl.BlockSpec((chunk, wv), tok),
                pl.BlockSpec((chunk, wv), tok),
                pl.BlockSpec((None, BF16_SUBLANES, chunk), lambda b, c: (b * nc + c, 0, 0)),
                pl.BlockSpec((chunk, LANES), tok),
                pl.BlockSpec((None, ML_HEADS, ML_DV, ML_DK), st4),
                pl.BlockSpec((None, ML_HEADS, ML_DK), st3),
                pl.BlockSpec((None, ML_HEADS, LANES), st3)]
    out_specs = (pl.BlockSpec((chunk, wv), tok0),
                 pl.BlockSpec((None, ML_HEADS, ML_DV, ML_DK), st4),
                 pl.BlockSpec((None, ML_HEADS, ML_DK), st3),
                 pl.BlockSpec((None, ML_HEADS, LANES), st3))
    scratch = [pltpu.VMEM((ML_HEADS, ML_DV, ML_DK), F32), pltpu.VMEM((ML_HEADS, ML_DK), F32),
               pltpu.VMEM((ML_HEADS, LANES), F32)]
    return pl.pallas_call(
        functools.partial(_mlstm_kernel, chunk=chunk),
        grid=(batch, nc), in_specs=in_specs, out_specs=out_specs, out_shape=out_shape,
        scratch_shapes=scratch, compiler_params=_params("arbitrary", "arbitrary"), name="mlstm",
    )(q, k, v, og, grow3, gcol, c0, n0, m0)


def _lambda(lamv_ref):
    lv = lamv_ref[...]
    s1 = jnp.sum(lv[0:1, :] * lv[1:2, :], axis=-1, keepdims=True)
    s2 = jnp.sum(lv[2:3, :] * lv[3:4, :], axis=-1, keepdims=True)
    return jnp.exp(s1) - jnp.exp(s2) + LAM_INIT


def _subln(o, gsub_ref):
    return _rms(o, SUBLN_EPS) * gsub_ref[...] * (1.0 - LAM_INIT)


def _online_softmax_step(s, v, m_s, l_s, acc_s):
    m_prev = m_s[...]
    m_new = jnp.maximum(m_prev, jnp.max(s, axis=-1, keepdims=True))
    alpha = jnp.exp(m_prev - m_new)
    p = jnp.exp(s - m_new)
    l_s[...] = alpha * l_s[...] + jnp.sum(p, axis=-1, keepdims=True)
    acc_s[...] = alpha * acc_s[...] + _dot(p.astype(BF16), v)
    m_s[...] = m_new


def _attn_kernel(qt_ref, kt_ref, q_ref, k_ref, v_ref, lamv_ref, gsub_ref, o_ref, q2_s, m_s, l_s, acc_s, *, blk):
    p = pl.program_id(2)
    qi = qt_ref[p]
    ki = kt_ref[p]

    @pl.when(ki == 0)
    def _():
        q = q_ref[...]
        lane = lax.broadcasted_iota(I32, q.shape, 1)
        zero = jnp.zeros_like(q)
        q2_s[0:blk, :] = jnp.where(lane < DA_QK, q, zero)
        q2_s[blk:2 * blk, :] = jnp.where(lane >= DA_QK, q, zero)
        m_s[...] = jnp.full(m_s.shape, -jnp.inf, F32)
        l_s[...] = jnp.zeros(l_s.shape, F32)
        acc_s[...] = jnp.zeros(acc_s.shape, F32)

    s = _dot(q2_s[...], k_ref[...])

    @pl.when(ki < qi)
    def _():
        _online_softmax_step(s, v_ref[...], m_s, l_s, acc_s)

    @pl.when(ki == qi)
    def _():
        r = lax.broadcasted_iota(I32, s.shape, 0) % blk
        cidx = lax.broadcasted_iota(I32, s.shape, 1)
        _online_softmax_step(jnp.where(cidx <= r, s, -jnp.inf), v_ref[...], m_s, l_s, acc_s)
        o2 = acc_s[...] / l_s[...]
        o = o2[0:blk, :] - _lambda(lamv_ref) * o2[blk:2 * blk, :]
        o_ref[...] = _subln(o, gsub_ref).astype(BF16)


def _attn_prompt(qa, ktb, vab, lamv, gsub, *, batch, seq, blk):
    nq = seq // blk
    pairs = [(qi, ki) for qi in range(nq) for ki in range(qi + 1)]
    qt = jnp.asarray([p[0] for p in pairs], I32)
    kt = jnp.asarray([p[1] for p in pairs], I32)
    hw = 2 * DA_QK

    def q_idx(b, h, p, qt, kt):
        return (b * nq + qt[p], h)

    def k_idx(b, h, p, qt, kt):
        return (b * DA_HEADS + h, kt[p])

    def v_idx(b, h, p, qt, kt):
        return (b * nq + kt[p], h)

    def const(b, h, p, qt, kt):
        return (0, 0)

    grid_spec = pltpu.PrefetchScalarGridSpec(
        num_scalar_prefetch=2, grid=(batch, DA_HEADS, len(pairs)),
        in_specs=[pl.BlockSpec((blk, hw), q_idx), pl.BlockSpec((hw, blk), k_idx), pl.BlockSpec((blk, DA_V), v_idx),
                  pl.BlockSpec(lamv.shape, const), pl.BlockSpec((1, DA_V), const)],
        out_specs=pl.BlockSpec((blk, DA_V), q_idx),
        scratch_shapes=[pltpu.VMEM((2 * blk, hw), BF16), pltpu.VMEM((2 * blk, 1), F32),
                        pltpu.VMEM((2 * blk, 1), F32), pltpu.VMEM((2 * blk, DA_V), F32)])
    return pl.pallas_call(
        functools.partial(_attn_kernel, blk=blk), grid_spec=grid_spec,
        out_shape=jax.ShapeDtypeStruct((batch * seq, DA_HEADS * DA_V), BF16),
        compiler_params=_params("arbitrary", "arbitrary", "arbitrary"), name="attn_prompt",
    )(qt, kt, qa, ktb, vab, lamv, gsub)


def _dec_attn_kernel(pt_ref, q_ref, *refs, pages, n_new):
    k_refs = refs[:pages]
    v_refs = refs[pages:2 * pages]
    kn_ref, vn_ref, lamv_ref, gsub_ref, o_ref, m_s, l_s, acc_s = refs[2 * pages:]
    j = pl.program_id(1)
    nj = pl.num_programs(1)

    @pl.when(j == 0)
    def _():
        m_s[...] = jnp.full(m_s.shape, -jnp.inf, F32)
        l_s[...] = jnp.zeros(l_s.shape, F32)
        acc_s[...] = jnp.zeros(acc_s.shape, F32)

    page = k_refs[0].shape[1]
    half_rows = DA_HEADS * SUBLANES
    q = q_ref[...]

    def v_page(vr):
        return jnp.concatenate([vr[pl.ds(h, page, stride=DA_HEADS), :] for h in range(DA_HEADS)], axis=1).astype(BF16)

    s = jnp.concatenate([_dot(q, k_refs[p][...].astype(BF16)) for p in range(pages)], axis=1)
    m_prev = m_s[...]
    m_new = jnp.maximum(m_prev, jnp.max(s, axis=-1, keepdims=True))
    alpha = jnp.exp(m_prev - m_new)
    pr = jnp.exp(s - m_new)
    pv = sum(_dot(pr[:, p * page:(p + 1) * page].astype(BF16), v_page(v_refs[p])) for p in range(pages))
    l_s[...] = alpha * l_s[...] + jnp.sum(pr, axis=-1, keepdims=True)
    acc_s[...] = alpha * acc_s[...] + pv
    m_s[...] = m_new

    @pl.when(j == nj - 1)
    def _():
        sn = _dot(q, kn_ref[...])
        t = jnp.minimum(lax.broadcasted_iota(I32, sn.shape, 0) % SUBLANES, n_new - 1)
        cidx = lax.broadcasted_iota(I32, sn.shape, 1)
        _online_softmax_step(jnp.where(cidx <= t, sn, -jnp.inf), vn_ref[...], m_s, l_s, acc_s)
        o2 = acc_s[...] / l_s[...]
        lam = _lambda(lamv_ref)
        outs = []
        for h in range(DA_HEADS):
            r0 = h * SUBLANES
            o0 = o2[r0:r0 + SUBLANES, h * DA_V:(h + 1) * DA_V]
            o1 = o2[half_rows + r0:half_rows + r0 + SUBLANES, h * DA_V:(h + 1) * DA_V]
            outs.append(_subln(o0 - lam * o1, gsub_ref))
        o_ref[...] = jnp.concatenate(outs, axis=1)


def _attn_sample(page_table, qbd, cache_kt, cache_v2, knt, vn, lamv, gsub, *, n_new, page):
    bs, npg = page_table.shape
    pages = min(PAGES_PER_STEP, npg)
    while npg % pages:
        pages -= 1
    rows, width = qbd.shape[1], qbd.shape[2]
    vrows = page * DA_HEADS

    def k_spec(p):
        return pl.BlockSpec((width, page), lambda b, j, pt: (pt[b, j * pages + p], 0))

    def v_spec(p):
        return pl.BlockSpec((vrows, DA_V), lambda b, j, pt: (pt[b, j * pages + p], 0))

    def seq3(b, j, pt):
        return (b, 0, 0)

    def const(b, j, pt):
        return (0, 0)

    in_specs = ([pl.BlockSpec((None, rows, width), seq3)]
                + [k_spec(p) for p in range(pages)] + [v_spec(p) for p in range(pages)]
                + [pl.BlockSpec((None, width, NEW_KV_PAD), seq3), pl.BlockSpec((None, NEW_KV_PAD, width), seq3),
                   pl.BlockSpec(lamv.shape, const), pl.BlockSpec((1, DA_V), const)])
    grid_spec = pltpu.PrefetchScalarGridSpec(
        num_scalar_prefetch=1, grid=(bs, npg // pages), in_specs=in_specs,
        out_specs=pl.BlockSpec((None, SUBLANES, width), seq3),
        scratch_shapes=[pltpu.VMEM((rows, 1), F32), pltpu.VMEM((rows, 1), F32), pltpu.VMEM((rows, width), F32)])
    return pl.pallas_call(
        functools.partial(_dec_attn_kernel, pages=pages, n_new=n_new), grid_spec=grid_spec,
        out_shape=jax.ShapeDtypeStruct((bs, SUBLANES, width), F32),
        compiler_params=_params("arbitrary", "arbitrary"), name="attn_sample",
    )(page_table, qbd, *([cache_kt] * pages), *([cache_v2] * pages), knt, vn, lamv, gsub)


def _merge_kernel(xp_ref, xs_ref, hgp_ref, hgs_ref, op_ref, os_ref, sga_ref, sgb_ref, wa_ref, wb_ref, wo_ref,
                  gffn_ref, wr_ref, br_ref, x1_ref, lg_ref, *, n_prompt_tiles):
    is_prompt = pl.program_id(0) < n_prompt_tiles
    x = jnp.where(is_prompt, xp_ref[...], xs_ref[...])
    hg = jnp.where(is_prompt, hgp_ref[...], hgs_ref[...])
    o = jnp.where(is_prompt, op_ref[...], os_ref[...])
    mixed = sga_ref[...].astype(F32) * _dot(hg, wa_ref[...]) + sgb_ref[...].astype(F32) * _dot(o, wb_ref[...])
    x1 = x + _dot(mixed.astype(BF16), wo_ref[...])
    _slab_store(x1_ref, x1, x1.shape[1] // LANES)
    xn = (_rms(x1, NORM_EPS) * gffn_ref[...]).astype(BF16)
    lg_ref[...] = _dot_nt(wr_ref[...], xn) + br_ref[...]


def _merge(x_p, x_s, hg_p, hg_s, o_p, o_s, sga, sgb, wa, wb, wo, g_ffn, wr, br, tm):
    tp, d = x_p.shape
    ts = x_s.shape[0]
    npt, nst = tp // tm, ts // tm
    t_all = tp + ts

    def tok(i):
        return (i, 0)

    def const(i):
        return (0, 0)

    def p_idx(i):
        return (jnp.minimum(i, npt - 1), 0)

    def s_idx(i):
        return (jnp.maximum(i - npt, 0), 0)

    wv, wo_in = hg_p.shape[1], o_p.shape[1]
    in_specs = [pl.BlockSpec((tm, d), p_idx), pl.BlockSpec((tm, d), s_idx),
                pl.BlockSpec((tm, wv), p_idx), pl.BlockSpec((tm, wv), s_idx),
                pl.BlockSpec((tm, wo_in), p_idx), pl.BlockSpec((tm, wo_in), s_idx),
                pl.BlockSpec((tm, d), tok), pl.BlockSpec((tm, d), tok),
                pl.BlockSpec(wa.shape, const), pl.BlockSpec(wb.shape, const), pl.BlockSpec(wo.shape, const),
                pl.BlockSpec((1, d), const), pl.BlockSpec(wr.shape, const), pl.BlockSpec((ROUTER_ROWS, 1), const)]
    ch = d // LANES
    out_shape = (jax.ShapeDtypeStruct((t_all * ch, LANES), F32), jax.ShapeDtypeStruct((ROUTER_ROWS, t_all), F32))
    out_specs = (pl.BlockSpec((tm * ch, LANES), tok), pl.BlockSpec((ROUTER_ROWS, tm), lambda i: (0, i)))
    return pl.pallas_call(
        functools.partial(_merge_kernel, n_prompt_tiles=npt),
        grid=(npt + nst,), in_specs=in_specs, out_specs=out_specs, out_shape=out_shape,
        compiler_params=_params("arbitrary"), name="merge",
    )(x_p, x_s, hg_p, hg_s, o_p, o_s, sga, sgb, wa, wb, wo, g_ffn, wr, br)


def _route_kernel(lg_ref, eid_ref, gw_ref):
    x = lg_ref[...]
    sub = lax.broadcasted_iota(I32, (SUBLANES, x.shape[1]), 0)
    lg = jnp.where(sub < MOE_GROUPS, x[0:SUBLANES, :], -jnp.inf)
    gmax = jnp.max(lg, axis=0, keepdims=True)
    g_star = jnp.min(jnp.where(lg == gmax, sub, SUBLANES), axis=0, keepdims=True)
    pg_top = 1.0 / jnp.sum(jnp.exp(lg - gmax), axis=0, keepdims=True)
    le = x[SUBLANES:2 * SUBLANES, :]
    for g in range(1, MOE_GROUPS):
        le = jnp.where(g_star == g, x[(g + 1) * SUBLANES:(g + 2) * SUBLANES, :], le)
    ex = jnp.exp(le - jnp.max(le, axis=0, keepdims=True))
    pe = ex / jnp.sum(ex, axis=0, keepdims=True)
    v1 = jnp.max(pe, axis=0, keepdims=True)
    i1 = jnp.min(jnp.where(pe == v1, sub, SUBLANES), axis=0, keepdims=True)
    rest = jnp.where(sub == i1, -jnp.inf, pe)
    v2 = jnp.max(rest, axis=0, keepdims=True)
    i2 = jnp.min(jnp.where(rest == v2, sub, SUBLANES), axis=0, keepdims=True)
    tot = v1 + v2
    e1 = g_star * MOE_PER_GROUP + i1
    e2 = g_star * MOE_PER_GROUP + i2
    w1 = pg_top * (v1 / tot)
    w2 = pg_top * (v2 / tot)
    eid_ref[...] = jnp.where(sub == 0, e1, jnp.where(sub == 1, e2, 0))
    gw_ref[...] = jnp.where(sub == 0, w1, jnp.where(sub == 1, w2, 0.0))


def _route(lg):
    rows, t_all = lg.shape
    tb = ROUTE_TILE
    while t_all % tb:
        tb //= 2
    spec = pl.BlockSpec((SUBLANES, tb), lambda i: (0, i))
    return pl.pallas_call(
        _route_kernel, grid=(t_all // tb,),
        in_specs=[pl.BlockSpec((rows, tb), lambda i: (0, i))], out_specs=(spec, spec),
        out_shape=(jax.ShapeDtypeStruct((SUBLANES, t_all), I32), jax.ShapeDtypeStruct((SUBLANES, t_all), F32)),
        compiler_params=_params("arbitrary"), name="route",
    )(lg)


def _expert_kernel(iblk_ref, iexp_ref, ilo_ref, ihi_ref, nit_ref, idx_hbm, x_hbm, w_ref, gffn_ref, wg_ref, wu_ref, wd_ref,
                   y_hbm, idx_s, xbuf, ybuf, wg_s, wu_s, wd_s, sem_i, sem_g, sem_s, *, rows, n_blocks, chunks):
    i = pl.program_id(0)
    valid = i < nit_ref[0]
    blk = iblk_ref[i]
    lo = ilo_ref[i]
    hi = ihi_ref[i]
    first = lo == 0
    last = hi == rows
    xs = blk % 2

    def idx_copy(b):
        return pltpu.make_async_copy(idx_hbm.at[b], idx_s.at[pl.ds((b % 3) * 2 * rows, 2 * rows)], sem_i.at[b % 3])

    def issue_gather(b):
        slot = b % 2
        base = (b % 3) * 2 * rows

        def body(r, carry):
            src = pl.multiple_of(idx_s[base + r] * chunks, chunks)
            pltpu.make_async_copy(x_hbm.at[pl.ds(src, chunks)], xbuf.at[slot, pl.ds(pl.multiple_of(r * chunks, chunks), chunks)],
                                  sem_g.at[slot]).start()
            return carry

        lax.fori_loop(0, rows, body, 0, unroll=DMA_ISSUE_UNROLL)

    def wait_gather(slot):
        pltpu.make_async_copy(x_hbm.at[pl.ds(0, rows * chunks)], xbuf.at[slot], sem_g.at[slot]).wait()

    def issue_scatter(b):
        slot = b % 2
        base = (b % 3) * 2 * rows + rows

        def body(r, carry):
            dst = pl.multiple_of(idx_s[base + r] * chunks, chunks)
            pltpu.make_async_copy(ybuf.at[slot, pl.ds(pl.multiple_of(r * chunks, chunks), chunks)], y_hbm.at[pl.ds(dst, chunks)],
                                  sem_s.at[slot]).start()
            return carry

        lax.fori_loop(0, rows, body, 0, unroll=DMA_ISSUE_UNROLL)

    def wait_scatter(slot):
        pltpu.make_async_copy(ybuf.at[slot], y_hbm.at[pl.ds(0, rows * chunks)], sem_s.at[slot]).wait()

    @pl.when(jnp.logical_and(valid, i == 0))
    def _():
        idx_copy(0).start()
        idx_copy(0).wait()
        if n_blocks > 1:
            idx_copy(1).start()
        issue_gather(0)

    @pl.when(jnp.logical_and(valid, first))
    def _():
        wait_gather(xs)

        @pl.when(blk + 1 < n_blocks)
        def _():
            idx_copy(blk + 1).wait()
            issue_gather(blk + 1)

        @pl.when(blk + 2 < n_blocks)
        def _():
            idx_copy(blk + 2).start()

        @pl.when(blk >= 2)
        def _():
            wait_scatter(xs)

    changed = jnp.logical_or(i == 0, iexp_ref[i] != iexp_ref[jnp.maximum(i - 1, 0)])

    @pl.when(jnp.logical_and(valid, changed))
    def _():
        wg_s[...] = wg_ref[...].astype(BF16)
        wu_s[...] = wu_ref[...].astype(BF16)
        wd_s[...] = wd_ref[...].astype(BF16)

    def compute():
        xn = (_rms(_slab_load(xbuf, rows, chunks, lead=xs), NORM_EPS) * gffn_ref[...]).astype(BF16)
        g = _dot(xn, wg_s[...])
        u = _dot(xn, wu_s[...])
        hmid = (g * _sigmoid(g) * u).astype(BF16)
        r = lax.broadcasted_iota(I32, (rows, 1), 0)
        wrow = jnp.where(jnp.logical_and(r >= lo, r < hi), w_ref[...], 0.0)
        return _dot(hmid, wd_s[...]) * wrow

    @pl.when(jnp.logical_and(valid, first))
    def _():
        _slab_store(ybuf, compute(), chunks, lead=xs)

    @pl.when(jnp.logical_and(valid, jnp.logical_not(first)))
    def _():
        _slab_store(ybuf, _slab_load(ybuf, rows, chunks, lead=xs) + compute(), chunks, lead=xs)

    @pl.when(jnp.logical_and(valid, last))
    def _():
        issue_scatter(blk)

    @pl.when(jnp.logical_and(valid, i == nit_ref[0] - 1))
    def _():
        wait_scatter(xs)

        @pl.when(blk >= 1)
        def _():
            wait_scatter(1 - xs)


def _experts(item_blk, item_exp, item_lo, item_hi, n_items, idx_rows, x1_slab, w_rows, g_ffn, w_gate, w_up, w_down, rows):
    n_blocks = idx_rows.shape[0]
    d, ff = w_gate.shape[-2], w_gate.shape[-1]
    chunks = d // LANES
    n_out = MOE_TOP_K * x1_slab.shape[0]
    n_max = item_blk.shape[0]

    def blk_idx(i, ib, ie, il, ih, nt):
        return (ib[i], 0)

    def const(i, ib, ie, il, ih, nt):
        return (0, 0)

    def w_idx(i, ib, ie, il, ih, nt):
        return (ie[i], 0, 0)

    grid_spec = pltpu.PrefetchScalarGridSpec(
        num_scalar_prefetch=5, grid=(n_max,),
        in_specs=[pl.BlockSpec(memory_space=pl.ANY), pl.BlockSpec(memory_space=pl.ANY),
                  pl.BlockSpec((rows, 1), blk_idx), pl.BlockSpec((1, d), const),
                  pl.BlockSpec((None, d, ff), w_idx), pl.BlockSpec((None, d, ff), w_idx),
                  pl.BlockSpec((None, ff, d), w_idx)],
        out_specs=pl.BlockSpec(memory_space=pl.ANY),
        scratch_shapes=[pltpu.SMEM((3 * 2 * rows,), I32),
                        pltpu.VMEM((2, rows * chunks, LANES), F32), pltpu.VMEM((2, rows * chunks, LANES), F32),
                        pltpu.VMEM((d, ff), BF16), pltpu.VMEM((d, ff), BF16), pltpu.VMEM((ff, d), BF16),
                        pltpu.SemaphoreType.DMA((3,)), pltpu.SemaphoreType.DMA((2,)), pltpu.SemaphoreType.DMA((2,))])
    return pl.pallas_call(
        functools.partial(_expert_kernel, rows=rows, n_blocks=n_blocks, chunks=chunks), grid_spec=grid_spec,
        out_shape=jax.ShapeDtypeStruct((n_out, LANES), F32),
        compiler_params=_params("arbitrary"), name="experts",
    )(item_blk, item_exp, item_lo, item_hi, n_items, idx_rows, x1_slab, w_rows, g_ffn, w_gate, w_up, w_down)


def _final_kernel(x1_ref, y0_ref, y1_ref, plep_ref, ples_ref, gple_ref, wpg_ref, wpp_ref, gfin_ref, yp_ref, ys_ref,
                  *, n_prompt_tiles):
    is_prompt = pl.program_id(0) < n_prompt_tiles
    tm, ch = plep_ref.shape[0], gple_ref.shape[1] // LANES
    x2 = _slab_load(x1_ref, tm, ch) + (_slab_load(y0_ref, tm, ch) + _slab_load(y1_ref, tm, ch))
    xn = (_rms(x2, NORM_EPS) * gple_ref[...]).astype(BF16)
    ple = jnp.where(is_prompt, plep_ref[...], ples_ref[...]).astype(BF16)
    x3 = x2 + _sigmoid(_dot(xn, wpg_ref[...])) * _dot(ple, wpp_ref[...])
    y = _rms(x3, NORM_EPS) * gfin_ref[...]

    @pl.when(is_prompt)
    def _():
        yp_ref[...] = y

    @pl.when(jnp.logical_not(is_prompt))
    def _():
        ys_ref[...] = y


def _final(x1, y_slots, ple_p, ple_s, g_ple, wpg, wpp, g_final, tm):
    d = g_ple.shape[1]
    ch = d // LANES
    tp, ts = ple_p.shape[0], ple_s.shape[0]
    npt, nst = tp // tm, ts // tm
    pd = ple_p.shape[1]
    nt = npt + nst

    def tok(i):
        return (i, 0)

    def tok1(i):
        return (nt + i, 0)

    def const(i):
        return (0, 0)

    def p_idx(i):
        return (jnp.minimum(i, npt - 1), 0)

    def s_idx(i):
        return (jnp.maximum(i - npt, 0), 0)

    in_specs = [pl.BlockSpec((tm * ch, LANES), tok), pl.BlockSpec((tm * ch, LANES), tok),
                pl.BlockSpec((tm * ch, LANES), tok1),
                pl.BlockSpec((tm, pd), p_idx), pl.BlockSpec((tm, pd), s_idx),
                pl.BlockSpec((1, d), const), pl.BlockSpec(wpg.shape, const), pl.BlockSpec(wpp.shape, const),
                pl.BlockSpec((1, d), const)]
    return pl.pallas_call(
        functools.partial(_final_kernel, n_prompt_tiles=npt),
        grid=(nt,), in_specs=in_specs,
        out_specs=(pl.BlockSpec((tm, d), p_idx), pl.BlockSpec((tm, d), s_idx)),
        out_shape=(jax.ShapeDtypeStruct((tp, d), F32), jax.ShapeDtypeStruct((ts, d), F32)),
        compiler_params=_params("arbitrary"), name="final",
    )(x1, y_slots, y_slots, ple_p, ple_s, g_ple, wpg, wpp, g_final)


def _rope_tables(pos):
    half = DA_QK // 2
    inv = ROPE_THETA ** (-jnp.arange(half, dtype=F32) / half)
    ang = pos.astype(F32)[:, None] * inv[None, :]
    cos, sin = jnp.cos(ang), jnp.sin(ang)
    reps = LANES // DA_QK
    cos_t = jnp.tile(jnp.concatenate([cos, cos], axis=1), (1, reps))
    sin_t = jnp.tile(jnp.concatenate([-sin, sin], axis=1), (1, reps))
    return cos_t, sin_t


def _tile(limit, *sizes):
    t = limit
    while any(s % t for s in sizes):
        t //= 2
    return t


def kernel(x_prompt, x_sample, cache_k, cache_v, state_mlstm_C, state_mlstm_n, state_mlstm_m, page_table, p_prompt, p_sample, g_mix, w_in, b_ml_i, b_ml_f, lam_q1, lam_k1, lam_q2, lam_k2, g_sub, w_br_a, w_br_b, w_out, g_ffn, w_rg, b_rg, w_re, b_re, w_e_gate, w_e_up, w_e_down, g_ple, w_ple_gate, w_ple_proj, g_final):
    depth = w_in.shape[0]
    assert depth == 1, "single-layer step"
    bp, sp, d = x_prompt.shape
    bs, ss, _ = x_sample.shape
    assert ss <= SUBLANES
    tp, ts = bp * sp, bs * ss
    t_all = tp + ts
    n_pages, page = page_table.shape[1], cache_k.shape[2]
    past_len = n_pages * page
    w = ML_HEADS * ML_DK
    aw = DA_HEADS * 2 * DA_QK
    li = 0

    wi = w_in[li]
    sizes = (w, w, ML_HEADS * ML_DV, ML_HEADS * ML_DV, ML_HEADS, ML_HEADS, aw, aw, DA_HEADS * DA_V, d, d)
    edges = [0]
    for n in sizes:
        edges.append(edges[-1] + n)
    assert edges[-1] == wi.shape[1]
    seg = [wi[:, edges[i]:edges[i + 1]] for i in range(11)]
    w_main = jnp.concatenate(seg[0:4] + [seg[6]] + seg[8:11], axis=1).astype(BF16)
    w_kt = seg[7].T.astype(BF16)
    w_gates = jnp.concatenate([seg[4], seg[5]], axis=1)
    w_gr = jnp.pad(w_gates.T, ((0, BF16_SUBLANES - 2 * ML_HEADS), (0, 0))).astype(BF16)
    w_gc = jnp.pad(w_gates, ((0, 0), (0, LANES - 2 * ML_HEADS))).astype(BF16)
    b_gates = jnp.concatenate([b_ml_i[li], b_ml_f[li]]).astype(F32)
    b_row = jnp.pad(b_gates, (0, BF16_SUBLANES - 2 * ML_HEADS))[:, None]
    b_col = jnp.pad(b_gates, (0, LANES - 2 * ML_HEADS))[None, :]
    lamv = jnp.stack([lam_q1[li], lam_k1[li], lam_q2[li], lam_k2[li]]).astype(F32)
    gsub = g_sub[li][None, :].astype(F32)
    w_router = jnp.zeros((ROUTER_ROWS, d), F32).at[0:MOE_GROUPS].set(w_rg[li].T).at[SUBLANES:SUBLANES + MOE_EXPERTS].set(w_re[li].T)
    b_router = jnp.zeros((ROUTER_ROWS,), F32).at[0:MOE_GROUPS].set(b_rg[li]).at[SUBLANES:SUBLANES + MOE_EXPERTS].set(b_re[li])

    tm = _tile(TOKEN_TILE, sp, ts)
    cos_p, sin_p = _rope_tables(jnp.arange(sp))
    cos_s, sin_s = _rope_tables(past_len + (jnp.arange(tm) % ss))
    cos_t = jnp.concatenate([cos_p, cos_s], axis=0)
    sin_t = jnp.concatenate([sin_p, sin_s], axis=0)
    cos_tt = cos_t[:, :DA_QK].T
    sin_tt = sin_t[:, :DA_QK].T
    xp2 = x_prompt.reshape(tp, d)
    xs2 = x_sample.reshape(ts, d)
    (q_ml, k_ml, v_ml, og, grow, gcol, qa, vab, sga, sgb, kt_p, ktb_p, v_p, kt_s, ktb_s, v_s) = _inproj(
        xp2, xs2, g_mix[li][None, :], w_main, w_kt, w_gr, w_gc, b_row, b_col, cos_t, sin_t, cos_tt, sin_tt, tm, sp)

    chunk = _tile(MLSTM_CHUNK, sp)
    ncp = sp // chunk
    grow3_p = grow[:, :tp].reshape(BF16_SUBLANES, bp * ncp, chunk).transpose(1, 0, 2)
    zc = jnp.zeros((bp, ML_HEADS, ML_DV, ML_DK), F32)
    zn = jnp.zeros((bp, ML_HEADS, ML_DK), F32)
    zm = jnp.zeros((bp, ML_HEADS, LANES), F32)
    hg_p, c_p, n_p, m_p = _mlstm(q_ml, k_ml, v_ml, og, grow3_p, gcol, zc, zn, zm,
                                 batch=bp, chunk=chunk, row_block_offset=0)

    padn = SAMPLE_PAD - ss

    def pad_seq(a):
        return jnp.pad(a[tp:].reshape(bs, ss, -1), ((0, 0), (0, padn), (0, 0))).reshape(bs * SAMPLE_PAD, -1)

    neutral = jnp.where(jnp.arange(LANES) < ML_HEADS, -1e30, 0.0).astype(F32)
    gcol_s = jnp.concatenate([gcol[tp:].reshape(bs, ss, LANES),
                              jnp.broadcast_to(neutral, (bs, padn, LANES))], axis=1).reshape(bs * SAMPLE_PAD, LANES)
    grow3_s = jnp.concatenate([grow[:, tp:].reshape(BF16_SUBLANES, bs, ss).transpose(1, 0, 2),
                               jnp.broadcast_to(neutral[:BF16_SUBLANES, None], (bs, BF16_SUBLANES, padn))], axis=2)
    m0_s = jnp.broadcast_to(state_mlstm_m[li].astype(F32)[:, :, None], (bs, ML_HEADS, LANES))
    hg_s_pad, c_s, n_s, m_s = _mlstm(pad_seq(q_ml), pad_seq(k_ml), pad_seq(v_ml), pad_seq(og), grow3_s, gcol_s,
                                     state_mlstm_C[li].astype(F32), state_mlstm_n[li].astype(F32), m0_s,
                                     batch=bs, chunk=SAMPLE_PAD, row_block_offset=0)
    hg_s = hg_s_pad.reshape(bs, SAMPLE_PAD, -1)[:, :ss].reshape(ts, -1)

    blk = _tile(ATTN_BLOCK, sp)
    o_p = _attn_prompt(qa, ktb_p, vab, lamv, gsub, batch=bp, seq=sp, blk=blk)

    q_s = jnp.pad(qa[tp:].reshape(bs, ss, DA_HEADS, 2, DA_QK), ((0, 0), (0, SUBLANES - ss), (0, 0), (0, 0), (0, 0)))
    q_cht = q_s.transpose(0, 3, 2, 1, 4)
    same = jnp.logical_and(
        (jnp.arange(DA_HEADS)[:, None] == jnp.arange(DA_HEADS)[None, :])[None, None, :, None, :, None, None],
        (jnp.arange(2)[:, None] == jnp.arange(2)[None, :])[None, :, None, None, None, :, None])
    qbd = jnp.where(same, q_cht[:, :, :, :, None, None, :], jnp.zeros((), BF16)).reshape(bs, 2 * DA_HEADS * SUBLANES, aw)
    knt = jnp.pad(ktb_s.reshape(aw, bs, ss).transpose(1, 0, 2), ((0, 0), (0, 0), (0, NEW_KV_PAD - ss)))
    vn = jnp.pad(vab[tp:].reshape(bs, ss, aw), ((0, 0), (0, NEW_KV_PAD - ss), (0, 0)))
    ckt = cache_k[li].transpose(0, 2, 3, 4, 1).reshape(-1, page)
    cv2 = cache_v[li].reshape(-1, DA_V)
    o_s = _attn_sample(page_table, qbd, ckt, cv2, knt, vn, lamv, gsub, n_new=ss, page=page)
    o_s = o_s[:, :ss].reshape(ts, aw).astype(BF16)

    x1, lg = _merge(xp2, xs2, hg_p, hg_s, o_p, o_s, sga, sgb,
                    w_br_a[li].astype(BF16), w_br_b[li].astype(BF16), w_out[li].astype(BF16),
                    g_ffn[li][None, :], w_router.astype(BF16), b_router[:, None], tm)

    eid8, gw8 = _route(lg)
    n_slots = t_all * MOE_TOP_K
    rows = _tile(EXPERT_ROWS, n_slots)
    nb = n_slots // rows
    flat_e = eid8[:MOE_TOP_K].T.reshape(-1)
    flat_w = gw8[:MOE_TOP_K].T.reshape(-1)
    slot_id = jnp.arange(n_slots, dtype=I32)
    sorted_e, order, sorted_w = lax.sort((flat_e, slot_id, flat_w), num_keys=1, is_stable=True)
    tok_rows = (order // MOE_TOP_K).reshape(nb, rows)
    dst_rows = ((order % MOE_TOP_K) * t_all + order // MOE_TOP_K).reshape(nb, rows)
    idx_rows = jnp.concatenate([tok_rows, dst_rows], axis=1)
    counts = jnp.sum(flat_e[None, :] == jnp.arange(MOE_EXPERTS, dtype=I32)[:, None], axis=1, dtype=I32)
    ends = jnp.cumsum(counts)
    starts = ends - counts
    first_blk = starts // rows
    n_e = jnp.where(counts > 0, (ends - 1) // rows - first_blk + 1, 0)
    item_end = jnp.cumsum(n_e)
    item_start = item_end - n_e
    n_items = item_end[-1:]
    n_max = nb + MOE_EXPERTS - 1
    it = jnp.minimum(jnp.arange(n_max, dtype=I32), n_items[0] - 1)
    item_exp = jnp.sum(item_end[None, :] <= it[:, None], axis=1, dtype=I32)
    item_blk = first_blk[item_exp] + it - item_start[item_exp]
    item_lo = jnp.maximum(starts[item_exp] - item_blk * rows, 0)
    item_hi = jnp.minimum(ends[item_exp] - item_blk * rows, rows)
    y_slots = _experts(item_blk, item_exp, item_lo, item_hi, n_items, idx_rows, x1, sorted_w[:, None],
                       g_ffn[li][None, :], w_e_gate[li], w_e_up[li], w_e_down[li], rows)

    pd = p_prompt.shape[-1]
    y_p, y_s = _final(x1, y_slots, p_prompt[li].reshape(tp, pd), p_sample[li].reshape(ts, pd), g_ple[li][None, :],
                      w_ple_gate[li].astype(BF16), w_ple_proj[li].astype(BF16), g_final[None, :], tm)

    return (y_p.reshape(bp, sp, d), y_s.reshape(bs, ss, d),
            kt_p.reshape(1, bp, DA_HEADS, 2, DA_QK, sp).transpose(0, 1, 5, 2, 3, 4), v_p.reshape(1, bp, sp, DA_HEADS, DA_V),
            c_p[None], n_p[None], m_p[None, :, :, 0],
            kt_s.reshape(1, DA_HEADS, 2, DA_QK, bs, ss).transpose(0, 4, 5, 1, 2, 3), v_s.reshape(1, bs, ss, DA_HEADS, DA_V),
            c_s[None], n_s[None], m_s[None, :, :, 0])
```

```python
import functools
import math

import jax
import jax.numpy as jnp
from jax import lax
from jax.experimental import pallas as pl
from jax.experimental.pallas import tpu as pltpu

F32 = jnp.float32
BF16 = jnp.bfloat16
I32 = jnp.int32

ML_HEADS = 4
ML_DK = 128
ML_DV = 128
DA_HEADS = 4
DA_QK = 64
DA_V = 128
ROPE_THETA = 10000.0
MOE_GROUPS = 4
MOE_PER_GROUP = 8
MOE_EXPERTS = MOE_GROUPS * MOE_PER_GROUP
MOE_TOP_K = 2
NORM_EPS = 1e-6
SUBLN_EPS = 1e-5
LAYER_INDEX = 0
LAM_INIT = 0.8 - 0.6 * math.exp(-0.3 * LAYER_INDEX)
LOG2E = math.log2(math.e)

LANES = 128
SUBLANES = 8
BF16_SUBLANES = 16
VMEM_LIMIT_BYTES = 56 * 1024 * 1024

TOKEN_TILE = 512
MLSTM_CHUNK = 256
ATTN_BLOCK = 512
PAGES_PER_STEP = 8
EXPERT_ROWS = 512
ROUTE_TILE = 512
ROUTER_ROWS = 48
SAMPLE_PAD = 16
NEW_KV_PAD = 128
DMA_ISSUE_UNROLL = 8

NT_DIMS = (((1,), (1,)), ((), ()))
TN_DIMS = (((0,), (0,)), ((), ()))


def _params(*sem):
    return pltpu.CompilerParams(dimension_semantics=sem, vmem_limit_bytes=VMEM_LIMIT_BYTES)


def _sigmoid(x):
    return 1.0 / (1.0 + jnp.exp(-x))


def _log_sigmoid(x):
    return jnp.minimum(x, 0.0) - jnp.log1p(jnp.exp(-jnp.abs(x)))


def _rms(x, eps):
    return x * lax.rsqrt(jnp.mean(x * x, axis=-1, keepdims=True) + eps)


def _dot(a, b):
    return jnp.dot(a, b, preferred_element_type=F32)


def _dot_nt(a, b):
    return lax.dot_general(a, b, NT_DIMS, preferred_element_type=F32)


def _dot_tn(a, b):
    return lax.dot_general(a, b, TN_DIMS, preferred_element_type=F32)


def _slab_load(ref, rows, chunks, lead=None):
    def piece(c):
        idx = (pl.ds(c, rows, stride=chunks), slice(None))
        return ref[idx] if lead is None else ref[(lead,) + idx]
    return jnp.concatenate([piece(c) for c in range(chunks)], axis=1)


def _slab_store(ref, val, chunks, lead=None):
    rows = val.shape[0]
    for c in range(chunks):
        idx = (pl.ds(c, rows, stride=chunks), slice(None))
        ref[idx if lead is None else (lead,) + idx] = val[:, c * LANES:(c + 1) * LANES]


def _split3(a):
    hi = a.astype(BF16)
    r1 = a - hi.astype(F32)
    mid = r1.astype(BF16)
    lo = (r1 - mid.astype(F32)).astype(BF16)
    return hi, mid, lo


def _inproj_kernel(xp_ref, xs_ref, g_ref, wm_ref, wkt_ref, wgr_ref, wgc_ref, br_ref, bc_ref, cos_ref, sin_ref,
                   cost_ref, sint_ref,
                   q_ref, k_ref, v_ref, og_ref, grow_ref, gcol_ref, qa_ref, vab_ref, sga_ref, sgb_ref,
                   ktp_ref, ktbp_ref, vp_ref, kts_ref, ktbs_ref, vs_ref, *, n_prompt_tiles):
    i = pl.program_id(0)
    is_prompt = i < n_prompt_tiles
    x = jnp.where(is_prompt, xp_ref[...], xs_ref[...])
    xn = (_rms(x, NORM_EPS) * g_ref[...]).astype(BF16)

    def mm(lo, hi):
        return _dot(xn, wm_ref[:, lo:hi])

    w = ML_HEADS * ML_DK
    q_ref[...] = mm(0, w).astype(BF16)
    k_ref[...] = (mm(w, 2 * w) * (ML_DK ** -0.5)).astype(BF16)
    v_ref[...] = mm(2 * w, 3 * w).astype(BF16)
    og_ref[...] = _sigmoid(mm(3 * w, 4 * w)).astype(BF16)

    gr = _dot_nt(wgr_ref[...], xn) + br_ref[...]
    rr = lax.broadcasted_iota(I32, gr.shape, 0)
    grow_ref[...] = jnp.where(rr >= ML_HEADS, _log_sigmoid(gr), gr)
    gc = _dot(xn, wgc_ref[...]) + bc_ref[...]
    cc = lax.broadcasted_iota(I32, gc.shape, 1)
    gcol_ref[...] = jnp.where(cc >= ML_HEADS, _log_sigmoid(gc), gc)

    aw = DA_HEADS * 2 * DA_QK
    reps = aw // LANES
    cosv = jnp.concatenate([cos_ref[...]] * reps, axis=1)
    sinv = jnp.concatenate([sin_ref[...]] * reps, axis=1)
    half = DA_QK // 2
    base = 4 * w
    zq = mm(base, base + aw)
    lane = lax.broadcasted_iota(I32, zq.shape, 1)
    partner = jnp.where((lane % DA_QK) < half, pltpu.roll(zq, aw - half, axis=1), pltpu.roll(zq, half, axis=1))
    qa_ref[...] = ((zq * cosv + partner * sinv) * (DA_QK ** -0.5 * LOG2E)).astype(BF16)

    kgroups = aw // DA_QK
    zk = _dot_nt(wkt_ref[...], xn)
    pieces = []
    for g in range(kgroups):
        pieces += [zk[g * DA_QK + half:(g + 1) * DA_QK, :], zk[g * DA_QK:g * DA_QK + half, :]]
    zk_partner = jnp.concatenate(pieces, axis=0)
    kt = (zk * jnp.concatenate([cost_ref[...]] * kgroups, axis=0)
          + zk_partner * jnp.concatenate([sint_ref[...]] * kgroups, axis=0))

    va = mm(base + aw, base + 2 * aw)
    vab_ref[...] = va.astype(BF16)
    tm = va.shape[0]

    def store_kv(kt_out, ktb_out, v_out):
        kt_out[...] = kt
        ktb_out[...] = kt.astype(BF16)
        for h in range(DA_HEADS):
            v_out[pl.ds(h, tm, stride=DA_HEADS), :] = va[:, h * DA_V:(h + 1) * DA_V]

    @pl.when(is_prompt)
    def _():
        store_kv(ktp_ref, ktbp_ref, vp_ref)

    @pl.when(jnp.logical_not(is_prompt))
    def _():
        store_kv(kts_ref, ktbs_ref, vs_ref)

    base = base + 2 * aw
    d = g_ref.shape[-1]
    sga_ref[...] = _sigmoid(mm(base, base + d)).astype(BF16)
    sgb_ref[...] = _sigmoid(mm(base + d, base + 2 * d)).astype(BF16)


def _inproj(x_p, x_s, g_mix, w_main, w_kt, w_gr, w_gc, b_row, b_col, cos_t, sin_t, cos_tt, sin_tt, tm, seq):
    tp, d = x_p.shape
    ts = x_s.shape[0]
    npt, nst = tp // tm, ts // tm
    n_pos_tiles = seq // tm
    batch = tp // seq
    t_all = tp + ts
    w = ML_HEADS * ML_DK
    aw = DA_HEADS * 2 * DA_QK
    ncols = w_main.shape[1]

    def tok(i):
        return (i, 0)

    def const(i):
        return (0, 0)

    def p_idx(i):
        return (jnp.minimum(i, npt - 1), 0)

    def s_idx(i):
        return (jnp.maximum(i - npt, 0), 0)

    def pos_idx(i):
        return (jnp.where(i < npt, i % n_pos_tiles, n_pos_tiles), 0)

    def pos_idx_t(i):
        return (0, jnp.where(i < npt, i % n_pos_tiles, n_pos_tiles))

    def ktp_idx(i):
        ip = jnp.minimum(i, npt - 1)
        return (ip // n_pos_tiles, ip % n_pos_tiles)

    def kts_idx(i):
        return (0, jnp.maximum(i - npt, 0))

    bf = lambda n: jax.ShapeDtypeStruct((t_all, n), BF16)
    out_shape = (bf(w), bf(w), bf(w), bf(w),
                 jax.ShapeDtypeStruct((BF16_SUBLANES, t_all), F32),
                 jax.ShapeDtypeStruct((t_all, LANES), F32),
                 bf(aw), bf(aw), bf(d), bf(d),
                 jax.ShapeDtypeStruct((batch * aw, seq), F32), jax.ShapeDtypeStruct((batch * aw, seq), BF16),
                 jax.ShapeDtypeStruct((tp * DA_HEADS, DA_V), F32),
                 jax.ShapeDtypeStruct((aw, ts), F32), jax.ShapeDtypeStruct((aw, ts), BF16),
                 jax.ShapeDtypeStruct((ts * DA_HEADS, DA_V), F32))
    out_specs = (pl.BlockSpec((tm, w), tok), pl.BlockSpec((tm, w), tok), pl.BlockSpec((tm, w), tok),
                 pl.BlockSpec((tm, w), tok),
                 pl.BlockSpec((BF16_SUBLANES, tm), lambda i: (0, i)),
                 pl.BlockSpec((tm, LANES), tok),
                 pl.BlockSpec((tm, aw), tok), pl.BlockSpec((tm, aw), tok),
                 pl.BlockSpec((tm, d), tok), pl.BlockSpec((tm, d), tok),
                 pl.BlockSpec((aw, tm), ktp_idx), pl.BlockSpec((aw, tm), ktp_idx),
                 pl.BlockSpec((tm * DA_HEADS, DA_V), p_idx),
                 pl.BlockSpec((aw, tm), kts_idx), pl.BlockSpec((aw, tm), kts_idx),
                 pl.BlockSpec((tm * DA_HEADS, DA_V), s_idx))
    in_specs = [pl.BlockSpec((tm, d), p_idx), pl.BlockSpec((tm, d), s_idx),
                pl.BlockSpec((1, d), const),
                pl.BlockSpec((d, ncols), const, pipeline_mode=pl.Buffered(1)),
                pl.BlockSpec((aw, d), const, pipeline_mode=pl.Buffered(1)),
                pl.BlockSpec((BF16_SUBLANES, d), const),
                pl.BlockSpec((d, LANES), const),
                pl.BlockSpec((BF16_SUBLANES, 1), const),
                pl.BlockSpec((1, LANES), const),
                pl.BlockSpec((tm, LANES), pos_idx), pl.BlockSpec((tm, LANES), pos_idx),
                pl.BlockSpec((DA_QK, tm), pos_idx_t), pl.BlockSpec((DA_QK, tm), pos_idx_t)]
    return pl.pallas_call(
        functools.partial(_inproj_kernel, n_prompt_tiles=npt),
        grid=(npt + nst,), in_specs=in_specs, out_specs=out_specs, out_shape=out_shape,
        compiler_params=_params("arbitrary"), name="inproj",
    )(x_p, x_s, g_mix, w_main, w_kt, w_gr, w_gc, b_row, b_col, cos_t, sin_t, cos_tt, sin_tt)


def _mlstm_kernel(q_ref, k_ref, v_ref, og_ref, grow_ref, gcol_ref, c0_ref, n0_ref, m0_ref,
                  hg_ref, c_out, n_out, m_out, c_s, n_s, m_s, *, chunk):
    c = pl.program_id(1)
    nc = pl.num_programs(1)

    @pl.when(c == 0)
    def _():
        c_s[...] = c0_ref[...]
        n_s[...] = n0_ref[...]
        m_s[...] = m0_ref[...]

    L = chunk
    row = lax.broadcasted_iota(I32, (L, L), 0)
    col = lax.broadcasted_iota(I32, (L, L), 1)
    causal = col <= row
    tri = causal.astype(BF16)
    tri_t = (row <= col).astype(BF16)

    g_row = grow_ref[...]
    g_col = gcol_ref[...]
    cum_row = sum(_dot(p, tri_t) for p in _split3(g_row))
    cum_col = sum(_dot(tri, p) for p in _split3(g_col))

    for h in range(ML_HEADS):
        lo, hi = h * ML_DK, (h + 1) * ML_DK
        f = ML_HEADS + h
        b_col = cum_col[:, f:f + 1]
        ig_col = g_col[:, h:h + 1]
        b_row = cum_row[f:f + 1, :]
        ig_row = g_row[h:h + 1, :]
        m0 = m_s[h:h + 1, 0:1]
        c0 = c_s[h]
        n0 = n_s[h:h + 1, :]
        qh = q_ref[:, lo:hi]
        kh = k_ref[:, lo:hi]
        vh = v_ref[:, h * ML_DV:(h + 1) * ML_DV]

        dmat = jnp.where(causal, b_col - b_row + ig_row, -jnp.inf)
        inter = b_col + m0
        mt = jnp.maximum(inter, jnp.max(dmat, axis=-1, keepdims=True))
        wts = jnp.exp(dmat - mt) * _dot_nt(qh, kh)
        decay0 = jnp.exp(inter - mt)
        num = _dot(wts.astype(BF16), vh) + decay0 * _dot_nt(qh, c0.astype(BF16))
        qn = jnp.sum(qh.astype(F32) * n0, axis=-1, keepdims=True)
        den = jnp.sum(wts, axis=-1, keepdims=True) + decay0 * qn
        hh = num / jnp.maximum(jnp.abs(den), jnp.exp(-mt))
        hg_ref[:, h * ML_DV:(h + 1) * ML_DV] = (hh * og_ref[:, h * ML_DV:(h + 1) * ML_DV].astype(F32)).astype(BF16)

        b_last = b_col[L - 1:L, :]
        m_new = mt[L - 1:L, :]
        g_last = jnp.exp(b_last + m0 - m_new)
        ws = jnp.exp(b_last - b_col + ig_col - m_new)
        vw = (vh.astype(F32) * ws).astype(BF16)
        c_s[h] = g_last * c0 + _dot_tn(vw, kh)
        n_s[h:h + 1, :] = g_last * n0 + jnp.sum(kh.astype(F32) * ws, axis=0, keepdims=True)
        m_s[h:h + 1, :] = jnp.broadcast_to(m_new, (1, LANES))

    @pl.when(c == nc - 1)
    def _():
        c_out[...] = c_s[...]
        n_out[...] = n_s[...]
        m_out[...] = m_s[...]


def _mlstm(q, k, v, og, grow3, gcol, c0, n0, m0, *, batch, chunk, row_block_offset):
    nchunks_total = grow3.shape[0]
    nc = nchunks_total // batch
    w = ML_HEADS * ML_DK
    wv = ML_HEADS * ML_DV

    def tok(b, c):
        return (row_block_offset + b * nc + c, 0)

    def tok0(b, c):
        return (b * nc + c, 0)

    def st4(b, c):
        return (b, 0, 0, 0)

    def st3(b, c):
        return (b, 0, 0)

    rows = batch * nc * chunk
    out_shape = (jax.ShapeDtypeStruct((rows, wv), BF16),
                 jax.ShapeDtypeStruct(c0.shape, F32),
                 jax.ShapeDtypeStruct(n0.shape, F32),
                 jax.ShapeDtypeStruct(m0.shape, F32))
    in_specs = [pl.BlockSpec((chunk, w), tok), pl.BlockSpec((chunk, w), tok), pl.BlockSpec((chunk, wv), tok),
                pl.BlockSpec((chunk, wv), tok),
                pl.BlockSpec((None, BF16_SUBLANES, chunk), lambda b, c: (b * nc + c, 0, 0)),
                pl.BlockSpec((chunk, LANES), tok),
                pl.BlockSpec((None, ML_HEADS, ML_DV, ML_DK), st4),
                pl.BlockSpec((None, ML_HEADS, ML_DK), st3),
                pl.BlockSpec((None, ML_HEADS, LANES), st3)]
    out_specs = (pl.BlockSpec((chunk, wv), tok0),
                 pl.BlockSpec((None, ML_HEADS, ML_DV, ML_DK), st4),
                 pl.BlockSpec((None, ML_HEADS, ML_DK), st3),
                 pl.BlockSpec((None, ML_HEADS, LANES), st3))
    scratch = [pltpu.VMEM((ML_HEADS, ML_DV, ML_DK), F32), pltpu.VMEM((ML_HEADS, ML_DK), F32),
               pltpu.VMEM((ML_HEADS, LANES), F32)]
    return pl.pallas_call(
        functools.partial(_mlstm_kernel, chunk=chunk),
        grid=(batch, nc), in_specs=in_specs, out_specs=out_specs, out_shape=out_shape,
        scratch_shapes=scratch, compiler_params=_params("arbitrary", "arbitrary"), name="mlstm",
    )(q, k, v, og, grow3, gcol, c0, n0, m0)


def _lambda(lamv_ref):
    lv = lamv_ref[...]
    s1 = jnp.sum(lv[0:1, :] * lv[1:2, :], axis=-1, keepdims=True)
    s2 = jnp.sum(lv[2:3, :] * lv[3:4, :], axis=-1, keepdims=True)
    return jnp.exp(s1) - jnp.exp(s2) + LAM_INIT


def _subln(o, gsub_ref):
    return _rms(o, SUBLN_EPS) * gsub_ref[...] * (1.0 - LAM_INIT)


def _lanes(a, width):
    reps = width // LANES
    return a if reps == 1 else jnp.concatenate([a] * reps, axis=1)


def _online_softmax_step(s, v, m_s, l_s, acc_s):
    m_prev = m_s[...]
    m_new = jnp.maximum(m_prev, jnp.max(s, axis=-1, keepdims=True))
    alpha = jnp.exp2(m_prev - m_new)
    p = jnp.exp2(s - _lanes(m_new, s.shape[1]))
    l_s[...] = alpha * l_s[...] + jnp.sum(p, axis=-1, keepdims=True)
    acc_s[...] = _lanes(alpha, acc_s.shape[1]) * acc_s[...] + _dot(p.astype(BF16), v)
    m_s[...] = m_new


def _attn_kernel(qt_ref, kt_ref, q_ref, k_ref, v_ref, lamv_ref, gsub_ref, o_ref, q2_s, m_s, l_s, acc_s, *, blk):
    p = pl.program_id(1)
    qi = qt_ref[p]
    ki = kt_ref[p]
    hw = 2 * DA_QK

    @pl.when(ki == 0)
    def _():
        q = q_ref[...]
        lane = lax.broadcasted_iota(I32, q.shape, 1) % hw
        zero = jnp.zeros_like(q)
        qlo = jnp.where(lane < DA_QK, q, zero)
        qhi = jnp.where(lane >= DA_QK, q, zero)
        for h in range(DA_HEADS):
            q2_s[h, 0:blk, :] = qlo[:, h * hw:(h + 1) * hw]
            q2_s[h, blk:2 * blk, :] = qhi[:, h * hw:(h + 1) * hw]
        m_s[...] = jnp.full(m_s.shape, -jnp.inf, F32)
        l_s[...] = jnp.zeros(l_s.shape, F32)
        acc_s[...] = jnp.zeros(acc_s.shape, F32)

    def scores(h):
        return _dot(q2_s[h], k_ref[h * hw:(h + 1) * hw, :])

    @pl.when(ki < qi)
    def _():
        for h in range(DA_HEADS):
            _online_softmax_step(scores(h), v_ref[:, h * DA_V:(h + 1) * DA_V], m_s.at[h], l_s.at[h], acc_s.at[h])

    @pl.when(ki == qi)
    def _():
        r = lax.broadcasted_iota(I32, (2 * blk, blk), 0) % blk
        cidx = lax.broadcasted_iota(I32, (2 * blk, blk), 1)
        lam = _lambda(lamv_ref)
        for h in range(DA_HEADS):
            _online_softmax_step(jnp.where(cidx <= r, scores(h), -jnp.inf), v_ref[:, h * DA_V:(h + 1) * DA_V],
                                 m_s.at[h], l_s.at[h], acc_s.at[h])
            o2 = acc_s[h] / l_s[h]
            o = o2[0:blk, :] - lam * o2[blk:2 * blk, :]
            o_ref[:, h * DA_V:(h + 1) * DA_V] = _subln(o, gsub_ref).astype(BF16)


def _attn_prompt(qa, ktb, vab, lamv, gsub, *, batch, seq, blk):
    nq = seq // blk
    pairs = [(qi, ki) for qi in range(nq) for ki in range(qi + 1)]
    qt = jnp.asarray([p[0] for p in pairs], I32)
    kt = jnp.asarray([p[1] for p in pairs], I32)
    hw = 2 * DA_QK
    width = DA_HEADS * hw

    def q_idx(b, p, qt, kt):
        return (b * nq + qt[p], 0)

    def k_idx(b, p, qt, kt):
        return (b, kt[p])

    def v_idx(b, p, qt, kt):
        return (b * nq + kt[p], 0)

    def const(b, p, qt, kt):
        return (0, 0)

    grid_spec = pltpu.PrefetchScalarGridSpec(
        num_scalar_prefetch=2, grid=(batch, len(pairs)),
        in_specs=[pl.BlockSpec((blk, width), q_idx), pl.BlockSpec((width, blk), k_idx),
                  pl.BlockSpec((blk, DA_HEADS * DA_V), v_idx),
                  pl.BlockSpec(lamv.shape, const), pl.BlockSpec((1, DA_V), const)],
        out_specs=pl.BlockSpec((blk, DA_HEADS * DA_V), q_idx),
        scratch_shapes=[pltpu.VMEM((DA_HEADS, 2 * blk, hw), BF16), pltpu.VMEM((DA_HEADS, 2 * blk, LANES), F32),
                        pltpu.VMEM((DA_HEADS, 2 * blk, LANES), F32), pltpu.VMEM((DA_HEADS, 2 * blk, DA_V), F32)])
    return pl.pallas_call(
        functools.partial(_attn_kernel, blk=blk), grid_spec=grid_spec,
        out_shape=jax.ShapeDtypeStruct((batch * seq, DA_HEADS * DA_V), BF16),
        compiler_params=_params("arbitrary", "arbitrary"), name="attn_prompt",
    )(qt, kt, qa, ktb, vab, lamv, gsub)


def _dec_attn_kernel(pt_ref, q_ref, *refs, pages, n_new):
    k_refs = refs[:pages]
    v_refs = refs[pages:2 * pages]
    kn_ref, vn_ref, lamv_ref, gsub_ref, o_ref, m_s, l_s, acc_s = refs[2 * pages:]
    j = pl.program_id(1)
    nj = pl.num_programs(1)

    @pl.when(j == 0)
    def _():
        m_s[...] = jnp.full(m_s.shape, -jnp.inf, F32)
        l_s[...] = jnp.zeros(l_s.shape, F32)
        acc_s[...] = jnp.zeros(acc_s.shape, F32)

    page = k_refs[0].shape[1]
    half_rows = DA_HEADS * SUBLANES
    q = q_ref[...]

    def v_page(vr):
        return jnp.concatenate([vr[pl.ds(h, page, stride=DA_HEADS), :] for h in range(DA_HEADS)], axis=1).astype(BF16)

    s = jnp.concatenate([_dot(q, k_refs[p][...].astype(BF16)) for p in range(pages)], axis=1)
    m_prev = m_s[...]
    m_new = jnp.maximum(m_prev, jnp.max(s, axis=-1, keepdims=True))
    alpha = jnp.exp2(m_prev - m_new)
    pr = jnp.exp2(s - _lanes(m_new, s.shape[1]))
    pv = sum(_dot(pr[:, p * page:(p + 1) * page].astype(BF16), v_page(v_refs[p])) for p in range(pages))
    l_s[...] = alpha * l_s[...] + jnp.sum(pr, axis=-1, keepdims=True)
    acc_s[...] = _lanes(alpha, acc_s.shape[1]) * acc_s[...] + pv
    m_s[...] = m_new

    @pl.when(j == nj - 1)
    def _():
        sn = _dot(q, kn_ref[...])
        t = jnp.minimum(lax.broadcasted_iota(I32, sn.shape, 0) % SUBLANES, n_new - 1)
        cidx = lax.broadcasted_iota(I32, sn.shape, 1)
        _online_softmax_step(jnp.where(cidx <= t, sn, -jnp.inf), vn_ref[...], m_s, l_s, acc_s)
        o2 = acc_s[...] / _lanes(l_s[...], acc_s.shape[1])
        lam = _lambda(lamv_ref)
        outs = []
        for h in range(DA_HEADS):
            r0 = h * SUBLANES
            o0 = o2[r0:r0 + SUBLANES, h * DA_V:(h + 1) * DA_V]
            o1 = o2[half_rows + r0:half_rows + r0 + SUBLANES, h * DA_V:(h + 1) * DA_V]
            outs.append(_subln(o0 - lam * o1, gsub_ref))
        o_ref[...] = jnp.concatenate(outs, axis=1)


def _attn_sample(page_table, qbd, cache_kt, cache_v2, knt, vn, lamv, gsub, *, n_new, page):
    bs, npg = page_table.shape
    pages = min(PAGES_PER_STEP, npg)
    while npg % pages:
        pages -= 1
    rows, width = qbd.shape[1], qbd.shape[2]
    vrows = page * DA_HEADS

    def k_spec(p):
        return pl.BlockSpec((width, page), lambda b, j, pt: (pt[b, j * pages + p], 0))

    def v_spec(p):
        return pl.BlockSpec((vrows, DA_V), lambda b, j, pt: (pt[b, j * pages + p], 0))

    def seq3(b, j, pt):
        return (b, 0, 0)

    def const(b, j, pt):
        return (0, 0)

    in_specs = ([pl.BlockSpec((None, rows, width), seq3)]
                + [k_spec(p) for p in range(pages)] + [v_spec(p) for p in range(pages)]
                + [pl.BlockSpec((None, width, NEW_KV_PAD), seq3), pl.BlockSpec((None, NEW_KV_PAD, width), seq3),
                   pl.BlockSpec(lamv.shape, const), pl.BlockSpec((1, DA_V), const)])
    grid_spec = pltpu.PrefetchScalarGridSpec(
        num_scalar_prefetch=1, grid=(bs, npg // pages), in_specs=in_specs,
        out_specs=pl.BlockSpec((None, SUBLANES, width), seq3),
        scratch_shapes=[pltpu.VMEM((rows, LANES), F32), pltpu.VMEM((rows, LANES), F32), pltpu.VMEM((rows, width), F32)])
    return pl.pallas_call(
        functools.partial(_dec_attn_kernel, pages=pages, n_new=n_new), grid_spec=grid_spec,
        out_shape=jax.ShapeDtypeStruct((bs, SUBLANES, width), F32),
        compiler_params=_params("arbitrary", "arbitrary"), name="attn_sample",
    )(page_table, qbd, *([cache_kt] * pages), *([cache_v2] * pages), knt, vn, lamv, gsub)


def _merge_kernel(xp_ref, xs_ref, hgp_ref, hgs_ref, op_ref, os_ref, sga_ref, sgb_ref, wa_ref, wb_ref, wo_ref,
                  gffn_ref, wr_ref, br_ref, x1_ref, lg_ref, *, n_prompt_tiles):
    is_prompt = pl.program_id(0) < n_prompt_tiles
    x = jnp.where(is_prompt, xp_ref[...], xs_ref[...])
    hg = jnp.where(is_prompt, hgp_ref[...], hgs_ref[...])
    o = jnp.where(is_prompt, op_ref[...], os_ref[...])
    mixed = sga_ref[...].astype(F32) * _dot(hg, wa_ref[...]) + sgb_ref[...].astype(F32) * _dot(o, wb_ref[...])
    x1 = x + _dot(mixed.astype(BF16), wo_ref[...])
    _slab_store(x1_ref, x1, x1.shape[1] // LANES)
    xn = (_rms(x1, NORM_EPS) * gffn_ref[...]).astype(BF16)
    lg_ref[...] = _dot_nt(wr_ref[...], xn) + br_ref[...]


def _merge(x_p, x_s, hg_p, hg_s, o_p, o_s, sga, sgb, wa, wb, wo, g_ffn, wr, br, tm):
    tp, d = x_p.shape
    ts = x_s.shape[0]
    npt, nst = tp // tm, ts // tm
    t_all = tp + ts

    def tok(i):
        return (i, 0)

    def const(i):
        return (0, 0)

    def p_idx(i):
        return (jnp.minimum(i, npt - 1), 0)

    def s_idx(i):
        return (jnp.maximum(i - npt, 0), 0)

    wv, wo_in = hg_p.shape[1], o_p.shape[1]
    in_specs = [pl.BlockSpec((tm, d), p_idx), pl.BlockSpec((tm, d), s_idx),
                pl.BlockSpec((tm, wv), p_idx), pl.BlockSpec((tm, wv), s_idx),
                pl.BlockSpec((tm, wo_in), p_idx), pl.BlockSpec((tm, wo_in), s_idx),
                pl.BlockSpec((tm, d), tok), pl.BlockSpec((tm, d), tok),
                pl.BlockSpec(wa.shape, const), pl.BlockSpec(wb.shape, const), pl.BlockSpec(wo.shape, const),
                pl.BlockSpec((1, d), const), pl.BlockSpec(wr.shape, const), pl.BlockSpec((ROUTER_ROWS, 1), const)]
    ch = d // LANES
    out_shape = (jax.ShapeDtypeStruct((t_all * ch, LANES), F32), jax.ShapeDtypeStruct((ROUTER_ROWS, t_all), F32))
    out_specs = (pl.BlockSpec((tm * ch, LANES), tok), pl.BlockSpec((ROUTER_ROWS, tm), lambda i: (0, i)))
    return pl.pallas_call(
        functools.partial(_merge_kernel, n_prompt_tiles=npt),
        grid=(npt + nst,), in_specs=in_specs, out_specs=out_specs, out_shape=out_shape,
        compiler_params=_params("arbitrary"), name="merge",
    )(x_p, x_s, hg_p, hg_s, o_p, o_s, sga, sgb, wa, wb, wo, g_ffn, wr, br)


def _route_kernel(lg_ref, eid_ref, gw_ref):
    x = lg_ref[...]
    sub = lax.broadcasted_iota(I32, (SUBLANES, x.shape[1]), 0)
    lg = jnp.where(sub < MOE_GROUPS, x[0:SUBLANES, :], -jnp.inf)
    gmax = jnp.max(lg, axis=0, keepdims=True)
    g_star = jnp.min(jnp.where(lg == gmax, sub, SUBLANES), axis=0, keepdims=True)
    pg_top = 1.0 / jnp.sum(jnp.exp(lg - gmax), axis=0, keepdims=True)
    le = x[SUBLANES:2 * SUBLANES, :]
    for g in range(1, MOE_GROUPS):
        le = jnp.where(g_star == g, x[(g + 1) * SUBLANES:(g + 2) * SUBLANES, :], le)
    ex = jnp.exp(le - jnp.max(le, axis=0, keepdims=True))
    pe = ex / jnp.sum(ex, axis=0, keepdims=True)
    v1 = jnp.max(pe, axis=0, keepdims=True)
    i1 = jnp.min(jnp.where(pe == v1, sub, SUBLANES), axis=0, keepdims=True)
    rest = jnp.where(sub == i1, -jnp.inf, pe)
    v2 = jnp.max(rest, axis=0, keepdims=True)
    i2 = jnp.min(jnp.where(rest == v2, sub, SUBLANES), axis=0, keepdims=True)
    tot = v1 + v2
    e1 = g_star * MOE_PER_GROUP + i1
    e2 = g_star * MOE_PER_GROUP + i2
    w1 = pg_top * (v1 / tot)
    w2 = pg_top * (v2 / tot)
    eid_ref[...] = jnp.where(sub == 0, e1, jnp.where(sub == 1, e2, 0))
    gw_ref[...] = jnp.where(sub == 0, w1, jnp.where(sub == 1, w2, 0.0))


def _route(lg):
    rows, t_all = lg.shape
    tb = ROUTE_TILE
    while t_all % tb:
        tb //= 2
    spec = pl.BlockSpec((SUBLANES, tb), lambda i: (0, i))
    return pl.pallas_call(
        _route_kernel, grid=(t_all // tb,),
        in_specs=[pl.BlockSpec((rows, tb), lambda i: (0, i))], out_specs=(spec, spec),
        out_shape=(jax.ShapeDtypeStruct((SUBLANES, t_all), I32), jax.ShapeDtypeStruct((SUBLANES, t_all), F32)),
        compiler_params=_params("arbitrary"), name="route",
    )(lg)


def _expert_kernel(iblk_ref, iexp_ref, ilo_ref, ihi_ref, nit_ref, idx_hbm, x_hbm, w_ref, gffn_ref, wg_ref, wu_ref, wd_ref,
                   y_hbm, idx_s, xbuf, ybuf, wg_s, wu_s, wd_s, sem_i, sem_g, sem_s, *, rows, n_blocks, chunks):
    i = pl.program_id(0)
    valid = i < nit_ref[0]
    blk = iblk_ref[i]
    lo = ilo_ref[i]
    hi = ihi_ref[i]
    first = lo == 0
    last = hi == rows
    xs = blk % 2

    def idx_copy(b):
        return pltpu.make_async_copy(idx_hbm.at[b], idx_s.at[pl.ds((b % 3) * 2 * rows, 2 * rows)], sem_i.at[b % 3])

    def issue_gather(b):
        slot = b % 2
        base = (b % 3) * 2 * rows

        def body(r, carry):
            src = pl.multiple_of(idx_s[base + r] * chunks, chunks)
            pltpu.make_async_copy(x_hbm.at[pl.ds(src, chunks)], xbuf.at[slot, pl.ds(pl.multiple_of(r * chunks, chunks), chunks)],
                                  sem_g.at[slot]).start()
            return carry

        lax.fori_loop(0, rows, body, 0, unroll=DMA_ISSUE_UNROLL)

    def wait_gather(slot):
        pltpu.make_async_copy(x_hbm.at[pl.ds(0, rows * chunks)], xbuf.at[slot], sem_g.at[slot]).wait()

    def issue_scatter(b):
        slot = b % 2
        base = (b % 3) * 2 * rows + rows

        def body(r, carry):
            dst = pl.multiple_of(idx_s[base + r] * chunks, chunks)
            pltpu.make_async_copy(ybuf.at[slot, pl.ds(pl.multiple_of(r * chunks, chunks), chunks)], y_hbm.at[pl.ds(dst, chunks)],
                                  sem_s.at[slot]).start()
            return carry

        lax.fori_loop(0, rows, body, 0, unroll=DMA_ISSUE_UNROLL)

    def wait_scatter(slot):
        pltpu.make_async_copy(ybuf.at[slot], y_hbm.at[pl.ds(0, rows * chunks)], sem_s.at[slot]).wait()

    @pl.when(jnp.logical_and(valid, i == 0))
    def _():
        idx_copy(0).start()
        idx_copy(0).wait()
        if n_blocks > 1:
            idx_copy(1).start()
        issue_gather(0)

    @pl.when(jnp.logical_and(valid, first))
    def _():
        wait_gather(xs)

        @pl.when(blk + 1 < n_blocks)
        def _():
            idx_copy(blk + 1).wait()
            issue_gather(blk + 1)

        @pl.when(blk + 2 < n_blocks)
        def _():
            idx_copy(blk + 2).start()

        @pl.when(blk >= 2)
        def _():
            wait_scatter(xs)

    changed = jnp.logical_or(i == 0, iexp_ref[i] != iexp_ref[jnp.maximum(i - 1, 0)])

    @pl.when(jnp.logical_and(valid, changed))
    def _():
        wg_s[...] = wg_ref[...].astype(BF16)
        wu_s[...] = wu_ref[...].astype(BF16)
        wd_s[...] = wd_ref[...].astype(BF16)

    def compute():
        xn = (_rms(_slab_load(xbuf, rows, chunks, lead=xs), NORM_EPS) * gffn_ref[...]).astype(BF16)
        g = _dot(xn, wg_s[...])
        u = _dot(xn, wu_s[...])
        hmid = (g * _sigmoid(g) * u).astype(BF16)
        r = lax.broadcasted_iota(I32, (rows, 1), 0)
        wrow = jnp.where(jnp.logical_and(r >= lo, r < hi), w_ref[...], 0.0)
        return _dot(hmid, wd_s[...]) * wrow

    @pl.when(jnp.logical_and(valid, first))
    def _():
        _slab_store(ybuf, compute(), chunks, lead=xs)

    @pl.when(jnp.logical_and(valid, jnp.logical_not(first)))
    def _():
        _slab_store(ybuf, _slab_load(ybuf, rows, chunks, lead=xs) + compute(), chunks, lead=xs)

    @pl.when(jnp.logical_and(valid, last))
    def _():
        issue_scatter(blk)

    @pl.when(jnp.logical_and(valid, i == nit_ref[0] - 1))
    def _():
        wait_scatter(xs)

        @pl.when(blk >= 1)
        def _():
            wait_scatter(1 - xs)


def _experts(item_blk, item_exp, item_lo, item_hi, n_items, idx_rows, x1_slab, w_rows, g_ffn, w_gate, w_up, w_down, rows):
    n_blocks = idx_rows.shape[0]
    d, ff = w_gate.shape[-2], w_gate.shape[-1]
    chunks = d // LANES
    n_out = MOE_TOP_K * x1_slab.shape[0]
    n_max = item_blk.shape[0]

    def blk_idx(i, ib, ie, il, ih, nt):
        return (ib[i], 0)

    def const(i, ib, ie, il, ih, nt):
        return (0, 0)

    def w_idx(i, ib, ie, il, ih, nt):
        return (ie[i], 0, 0)

    grid_spec = pltpu.PrefetchScalarGridSpec(
        num_scalar_prefetch=5, grid=(n_max,),
        in_specs=[pl.BlockSpec(memory_space=pl.ANY), pl.BlockSpec(memory_space=pl.ANY),
                  pl.BlockSpec((rows, 1), blk_idx), pl.BlockSpec((1, d), const),
                  pl.BlockSpec((None, d, ff), w_idx), pl.BlockSpec((None, d, ff), w_idx),
                  pl.BlockSpec((None, ff, d), w_idx)],
        out_specs=pl.BlockSpec(memory_space=pl.ANY),
        scratch_shapes=[pltpu.SMEM((3 * 2 * rows,), I32),
                        pltpu.VMEM((2, rows * chunks, LANES), F32), pltpu.VMEM((2, rows * chunks, LANES), F32),
                        pltpu.VMEM((d, ff), BF16), pltpu.VMEM((d, ff), BF16), pltpu.VMEM((ff, d), BF16),
                        pltpu.SemaphoreType.DMA((3,)), pltpu.SemaphoreType.DMA((2,)), pltpu.SemaphoreType.DMA((2,))])
    return pl.pallas_call(
        functools.partial(_expert_kernel, rows=rows, n_blocks=n_blocks, chunks=chunks), grid_spec=grid_spec,
        out_shape=jax.ShapeDtypeStruct((n_out, LANES), F32),
        compiler_params=_params("arbitrary"), name="experts",
    )(item_blk, item_exp, item_lo, item_hi, n_items, idx_rows, x1_slab, w_rows, g_ffn, w_gate, w_up, w_down)


def _final_kernel(x1_ref, y0_ref, y1_ref, plep_ref, ples_ref, gple_ref, wpg_ref, wpp_ref, gfin_ref, yp_ref, ys_ref,
                  *, n_prompt_tiles):
    is_prompt = pl.program_id(0) < n_prompt_tiles
    tm, ch = plep_ref.shape[0], gple_ref.shape[1] // LANES
    x2 = _slab_load(x1_ref, tm, ch) + (_slab_load(y0_ref, tm, ch) + _slab_load(y1_ref, tm, ch))
    xn = (_rms(x2, NORM_EPS) * gple_ref[...]).astype(BF16)
    ple = jnp.where(is_prompt, plep_ref[...], ples_ref[...]).astype(BF16)
    x3 = x2 + _sigmoid(_dot(xn, wpg_ref[...])) * _dot(ple, wpp_ref[...])
    y = _rms(x3, NORM_EPS) * gfin_ref[...]

    @pl.when(is_prompt)
    def _():
        yp_ref[...] = y

    @pl.when(jnp.logical_not(is_prompt))
    def _():
        ys_ref[...] = y


def _final(x1, y_slots, ple_p, ple_s, g_ple, wpg, wpp, g_final, tm):
    d = g_ple.shape[1]
    ch = d // LANES
    tp, ts = ple_p.shape[0], ple_s.shape[0]
    npt, nst = tp // tm, ts // tm
    pd = ple_p.shape[1]
    nt = npt + nst

    def tok(i):
        return (i, 0)

    def tok1(i):
        return (nt + i, 0)

    def const(i):
        return (0, 0)

    def p_idx(i):
        return (jnp.minimum(i, npt - 1), 0)

    def s_idx(i):
        return (jnp.maximum(i - npt, 0), 0)

    in_specs = [pl.BlockSpec((tm * ch, LANES), tok), pl.BlockSpec((tm * ch, LANES), tok),
                pl.BlockSpec((tm * ch, LANES), tok1),
                pl.BlockSpec((tm, pd), p_idx), pl.BlockSpec((tm, pd), s_idx),
                pl.BlockSpec((1, d), const), pl.BlockSpec(wpg.shape, const), pl.BlockSpec(wpp.shape, const),
                pl.BlockSpec((1, d), const)]
    return pl.pallas_call(
        functools.partial(_final_kernel, n_prompt_tiles=npt),
        grid=(nt,), in_specs=in_specs,
        out_specs=(pl.BlockSpec((tm, d), p_idx), pl.BlockSpec((tm, d), s_idx)),
        out_shape=(jax.ShapeDtypeStruct((tp, d), F32), jax.ShapeDtypeStruct((ts, d), F32)),
        compiler_params=_params("arbitrary"), name="final",
    )(x1, y_slots, y_slots, ple_p, ple_s, g_ple, wpg, wpp, g_final)


def _rope_tables(pos):
    half = DA_QK // 2
    inv = ROPE_THETA ** (-jnp.arange(half, dtype=F32) / half)
    ang = pos.astype(F32)[:, None] * inv[None, :]
    cos, sin = jnp.cos(ang), jnp.sin(ang)
    reps = LANES // DA_QK
    cos_t = jnp.tile(jnp.concatenate([cos, cos], axis=1), (1, reps))
    sin_t = jnp.tile(jnp.concatenate([-sin, sin], axis=1), (1, reps))
    return cos_t, sin_t


def _tile(limit, *sizes):
    t = limit
    while any(s % t for s in sizes):
        t //= 2
    return t


def kernel(x_prompt, x_sample, cache_k, cache_v, state_mlstm_C, state_mlstm_n, state_mlstm_m, page_table, p_prompt, p_sample, g_mix, w_in, b_ml_i, b_ml_f, lam_q1, lam_k1, lam_q2, lam_k2, g_sub, w_br_a, w_br_b, w_out, g_ffn, w_rg, b_rg, w_re, b_re, w_e_gate, w_e_up, w_e_down, g_ple, w_ple_gate, w_ple_proj, g_final):
    depth = w_in.shape[0]
    assert depth == 1, "single-layer step"
    bp, sp, d = x_prompt.shape
    bs, ss, _ = x_sample.shape
    assert ss <= SUBLANES
    tp, ts = bp * sp, bs * ss
    t_all = tp + ts
    n_pages, page = page_table.shape[1], cache_k.shape[2]
    past_len = n_pages * page
    w = ML_HEADS * ML_DK
    aw = DA_HEADS * 2 * DA_QK
    li = 0

    wi = w_in[li]
    sizes = (w, w, ML_HEADS * ML_DV, ML_HEADS * ML_DV, ML_HEADS, ML_HEADS, aw, aw, DA_HEADS * DA_V, d, d)
    edges = [0]
    for n in sizes:
        edges.append(edges[-1] + n)
    assert edges[-1] == wi.shape[1]
    seg = [wi[:, edges[i]:edges[i + 1]] for i in range(11)]
    w_main = jnp.concatenate(seg[0:4] + [seg[6]] + seg[8:11], axis=1).astype(BF16)
    w_kt = seg[7].T.astype(BF16)
    w_gates = jnp.concatenate([seg[4], seg[5]], axis=1)
    w_gr = jnp.pad(w_gates.T, ((0, BF16_SUBLANES - 2 * ML_HEADS), (0, 0))).astype(BF16)
    w_gc = jnp.pad(w_gates, ((0, 0), (0, LANES - 2 * ML_HEADS))).astype(BF16)
    b_gates = jnp.concatenate([b_ml_i[li], b_ml_f[li]]).astype(F32)
    b_row = jnp.pad(b_gates, (0, BF16_SUBLANES - 2 * ML_HEADS))[:, None]
    b_col = jnp.pad(b_gates, (0, LANES - 2 * ML_HEADS))[None, :]
    lamv = jnp.stack([lam_q1[li], lam_k1[li], lam_q2[li], lam_k2[li]]).astype(F32)
    gsub = g_sub[li][None, :].astype(F32)
    w_router = jnp.zeros((ROUTER_ROWS, d), F32).at[0:MOE_GROUPS].set(w_rg[li].T).at[SUBLANES:SUBLANES + MOE_EXPERTS].set(w_re[li].T)
    b_router = jnp.zeros((ROUTER_ROWS,), F32).at[0:MOE_GROUPS].set(b_rg[li]).at[SUBLANES:SUBLANES + MOE_EXPERTS].set(b_re[li])

    tm = _tile(TOKEN_TILE, sp, ts)
    cos_p, sin_p = _rope_tables(jnp.arange(sp))
    cos_s, sin_s = _rope_tables(past_len + (jnp.arange(tm) % ss))
    cos_t = jnp.concatenate([cos_p, cos_s], axis=0)
    sin_t = jnp.concatenate([sin_p, sin_s], axis=0)
    cos_tt = cos_t[:, :DA_QK].T
    sin_tt = sin_t[:, :DA_QK].T
    xp2 = x_prompt.reshape(tp, d)
    xs2 = x_sample.reshape(ts, d)
    (q_ml, k_ml, v_ml, og, grow, gcol, qa, vab, sga, sgb, kt_p, ktb_p, v_p, kt_s, ktb_s, v_s) = _inproj(
        xp2, xs2, g_mix[li][None, :], w_main, w_kt, w_gr, w_gc, b_row, b_col, cos_t, sin_t, cos_tt, sin_tt, tm, sp)

    chunk = _tile(MLSTM_CHUNK, sp)
    ncp = sp // chunk
    grow3_p = grow[:, :tp].reshape(BF16_SUBLANES, bp * ncp, chunk).transpose(1, 0, 2)
    zc = jnp.zeros((bp, ML_HEADS, ML_DV, ML_DK), F32)
    zn = jnp.zeros((bp, ML_HEADS, ML_DK), F32)
    zm = jnp.zeros((bp, ML_HEADS, LANES), F32)
    hg_p, c_p, n_p, m_p = _mlstm(q_ml, k_ml, v_ml, og, grow3_p, gcol, zc, zn, zm,
                                 batch=bp, chunk=chunk, row_block_offset=0)

    padn = SAMPLE_PAD - ss

    def pad_seq(a):
        return jnp.pad(a[tp:].reshape(bs, ss, -1), ((0, 0), (0, padn), (0, 0))).reshape(bs * SAMPLE_PAD, -1)

    neutral = jnp.where(jnp.arange(LANES) < ML_HEADS, -1e30, 0.0).astype(F32)
    gcol_s = jnp.concatenate([gcol[tp:].reshape(bs, ss, LANES),
                              jnp.broadcast_to(neutral, (bs, padn, LANES))], axis=1).reshape(bs * SAMPLE_PAD, LANES)
    grow3_s = jnp.concatenate([grow[:, tp:].reshape(BF16_SUBLANES, bs, ss).transpose(1, 0, 2),
                               jnp.broadcast_to(neutral[:BF16_SUBLANES, None], (bs, BF16_SUBLANES, padn))], axis=2)
    m0_s = jnp.broadcast_to(state_mlstm_m[li].astype(F32)[:, :, None], (bs, ML_HEADS, LANES))
    hg_s_pad, c_s, n_s, m_s = _mlstm(pad_seq(q_ml), pad_seq(k_ml), pad_seq(v_ml), pad_seq(og), grow3_s, gcol_s,
                                     state_mlstm_C[li].astype(F32), state_mlstm_n[li].astype(F32), m0_s,
                                     batch=bs, chunk=SAMPLE_PAD, row_block_offset=0)
    hg_s = hg_s_pad.reshape(bs, SAMPLE_PAD, -1)[:, :ss].reshape(ts, -1)

    blk = _tile(ATTN_BLOCK, sp)
    o_p = _attn_prompt(qa, ktb_p, vab, lamv, gsub, batch=bp, seq=sp, blk=blk)

    q_s = jnp.pad(qa[tp:].reshape(bs, ss, DA_HEADS, 2, DA_QK), ((0, 0), (0, SUBLANES - ss), (0, 0), (0, 0), (0, 0)))
    q_cht = q_s.transpose(0, 3, 2, 1, 4)
    same = jnp.logical_and(
        (jnp.arange(DA_HEADS)[:, None] == jnp.arange(DA_HEADS)[None, :])[None, None, :, None, :, None, None],
        (jnp.arange(2)[:, None] == jnp.arange(2)[None, :])[None, :, None, None, None, :, None])
    qbd = jnp.where(same, q_cht[:, :, :, :, None, None, :], jnp.zeros((), BF16)).reshape(bs, 2 * DA_HEADS * SUBLANES, aw)
    knt = jnp.pad(ktb_s.reshape(aw, bs, ss).transpose(1, 0, 2), ((0, 0), (0, 0), (0, NEW_KV_PAD - ss)))
    vn = jnp.pad(vab[tp:].reshape(bs, ss, aw), ((0, 0), (0, NEW_KV_PAD - ss), (0, 0)))
    ckt = cache_k[li].transpose(0, 2, 3, 4, 1).reshape(-1, page)
    cv2 = cache_v[li].reshape(-1, DA_V)
    o_s = _attn_sample(page_table, qbd, ckt, cv2, knt, vn, lamv, gsub, n_new=ss, page=page)
    o_s = o_s[:, :ss].reshape(ts, aw).astype(BF16)

    x1, lg = _merge(xp2, xs2, hg_p, hg_s, o_p, o_s, sga, sgb,
                    w_br_a[li].astype(BF16), w_br_b[li].astype(BF16), w_out[li].astype(BF16),
                    g_ffn[li][None, :], w_router.astype(BF16), b_router[:, None], tm)

    eid8, gw8 = _route(lg)
    n_slots = t_all * MOE_TOP_K
    rows = _tile(EXPERT_ROWS, n_slots)
    nb = n_slots // rows
    flat_e = eid8[:MOE_TOP_K].T.reshape(-1)
    flat_w = gw8[:MOE_TOP_K].T.reshape(-1)
    slot_id = jnp.arange(n_slots, dtype=I32)
    sorted_e, order, sorted_w = lax.sort((flat_e, slot_id, flat_w), num_keys=1, is_stable=True)
    tok_rows = (order // MOE_TOP_K).reshape(nb, rows)
    dst_rows = ((order % MOE_TOP_K) * t_all + order // MOE_TOP_K).reshape(nb, rows)
    idx_rows = jnp.concatenate([tok_rows, dst_rows], axis=1)
    counts = jnp.sum(flat_e[None, :] == jnp.arange(MOE_EXPERTS, dtype=I32)[:, None], axis=1, dtype=I32)
    ends = jnp.cumsum(counts)
    starts = ends - counts
    first_blk = starts // rows
    n_e = jnp.where(counts > 0, (ends - 1) // rows - first_blk + 1, 0)
    item_end = jnp.cumsum(n_e)
    item_start = item_end - n_e
    n_items = item_end[-1:]
    n_max = nb + MOE_EXPERTS - 1
    it = jnp.minimum(jnp.arange(n_max, dtype=I32), n_items[0] - 1)
    item_exp = jnp.sum(item_end[None, :] <= it[:, None], axis=1, dtype=I32)
    item_blk = first_blk[item_exp] + it - item_start[item_exp]
    item_lo = jnp.maximum(starts[item_exp] - item_blk * rows, 0)
    item_hi = jnp.minimum(ends[item_exp] - item_blk * rows, rows)
    y_slots = _experts(item_blk, item_exp, item_lo, item_hi, n_items, idx_rows, x1, sorted_w[:, None],
                       g_ffn[li][None, :], w_e_gate[li], w_e_up[li], w_e_down[li], rows)

    pd = p_prompt.shape[-1]
    y_p, y_s = _final(x1, y_slots, p_prompt[li].reshape(tp, pd), p_sample[li].reshape(ts, pd), g_ple[li][None, :],
                      w_ple_gate[li].astype(BF16), w_ple_proj[li].astype(BF16), g_final[None, :], tm)

    return (y_p.reshape(bp, sp, d), y_s.reshape(bs, ss, d),
            kt_p.reshape(1, bp, DA_HEADS, 2, DA_QK, sp).transpose(0, 1, 5, 2, 3, 4), v_p.reshape(1, bp, sp, DA_HEADS, DA_V),
            c_p[None], n_p[None], m_p[None, :, :, 0],
            kt_s.reshape(1, DA_HEADS, 2, DA_QK, bs, ss).transpose(0, 4, 5, 1, 2, 3), v_s.reshape(1, bs, ss, DA_HEADS, DA_V),
            c_s[None], n_s[None], m_s[None, :, :, 0])
```

```python
import functools
import math

import jax
import jax.numpy as jnp
from jax import lax
from jax.experimental import pallas as pl
from jax.experimental.pallas import tpu as pltpu

F32 = jnp.float32
BF16 = jnp.bfloat16
I32 = jnp.int32

ML_HEADS = 4
ML_DK = 128
ML_DV = 128
DA_HEADS = 4
DA_QK = 64
DA_V = 128
ROPE_THETA = 10000.0
MOE_GROUPS = 4
MOE_PER_GROUP = 8
MOE_EXPERTS = MOE_GROUPS * MOE_PER_GROUP
MOE_TOP_K = 2
NORM_EPS = 1e-6
SUBLN_EPS = 1e-5
LAYER_INDEX = 0
LAM_INIT = 0.8 - 0.6 * math.exp(-0.3 * LAYER_INDEX)
LOG2E = math.log2(math.e)

LANES = 128
SUBLANES = 8
BF16_SUBLANES = 16
VMEM_LIMIT_BYTES = 56 * 1024 * 1024

TOKEN_TILE = 512
MLSTM_CHUNK = 256
ATTN_BLOCK = 512
PAGES_PER_STEP = 8
EXPERT_ROWS = 512
ROUTE_TILE = 512
ROUTER_ROWS = 48
SAMPLE_PAD = 16
NEW_KV_PAD = 128
DMA_ISSUE_UNROLL = 8

NT_DIMS = (((1,), (1,)), ((), ()))
TN_DIMS = (((0,), (0,)), ((), ()))


def _params(*sem):
    return pltpu.CompilerParams(dimension_semantics=sem, vmem_limit_bytes=VMEM_LIMIT_BYTES)


def _sigmoid(x):
    return 1.0 / (1.0 + jnp.exp(-x))


def _log_sigmoid(x):
    return jnp.minimum(x, 0.0) - jnp.log1p(jnp.exp(-jnp.abs(x)))


def _rms(x, eps):
    return x * lax.rsqrt(jnp.mean(x * x, axis=-1, keepdims=True) + eps)


def _dot(a, b):
    return jnp.dot(a, b, preferred_element_type=F32)


def _dot_nt(a, b):
    return lax.dot_general(a, b, NT_DIMS, preferred_element_type=F32)


def _dot_tn(a, b):
    return lax.dot_general(a, b, TN_DIMS, preferred_element_type=F32)


def _slab_load(ref, rows, chunks, lead=None):
    def piece(c):
        idx = (pl.ds(c, rows, stride=chunks), slice(None))
        return ref[idx] if lead is None else ref[(lead,) + idx]
    return jnp.concatenate([piece(c) for c in range(chunks)], axis=1)


def _slab_store(ref, val, chunks, lead=None):
    rows = val.shape[0]
    for c in range(chunks):
        idx = (pl.ds(c, rows, stride=chunks), slice(None))
        ref[idx if lead is None else (lead,) + idx] = val[:, c * LANES:(c + 1) * LANES]


def _split3(a):
    hi = a.astype(BF16)
    r1 = a - hi.astype(F32)
    mid = r1.astype(BF16)
    lo = (r1 - mid.astype(F32)).astype(BF16)
    return hi, mid, lo


def _inproj_kernel(xp_ref, xs_ref, g_ref, wm_ref, wkt_ref, wgr_ref, wgc_ref, br_ref, bc_ref, cos_ref, sin_ref,
                   cost_ref, sint_ref,
                   q_ref, k_ref, v_ref, og_ref, grow_ref, gcol_ref, qa_ref, vab_ref, sga_ref, sgb_ref,
                   ktp_ref, ktbp_ref, vp_ref, kts_ref, ktbs_ref, vs_ref, *, n_prompt_tiles):
    i = pl.program_id(0)
    is_prompt = i < n_prompt_tiles
    x = jnp.where(is_prompt, xp_ref[...], xs_ref[...])
    xn = (_rms(x, NORM_EPS) * g_ref[...]).astype(BF16)

    def mm(lo, hi):
        return _dot(xn, wm_ref[:, lo:hi])

    w = ML_HEADS * ML_DK
    q_ref[...] = mm(0, w).astype(BF16)
    k_ref[...] = (mm(w, 2 * w) * (ML_DK ** -0.5)).astype(BF16)
    v_ref[...] = mm(2 * w, 3 * w).astype(BF16)
    og_ref[...] = _sigmoid(mm(3 * w, 4 * w)).astype(BF16)

    gr = _dot_nt(wgr_ref[...], xn) + br_ref[...]
    rr = lax.broadcasted_iota(I32, gr.shape, 0)
    grow_ref[...] = jnp.where(rr >= ML_HEADS, _log_sigmoid(gr), gr)
    gc = _dot(xn, wgc_ref[...]) + bc_ref[...]
    cc = lax.broadcasted_iota(I32, gc.shape, 1)
    gcol_ref[...] = jnp.where(cc >= ML_HEADS, _log_sigmoid(gc), gc)

    aw = DA_HEADS * 2 * DA_QK
    reps = aw // LANES
    cosv = jnp.concatenate([cos_ref[...]] * reps, axis=1)
    sinv = jnp.concatenate([sin_ref[...]] * reps, axis=1)
    half = DA_QK // 2
    base = 4 * w
    zq = mm(base, base + aw)
    lane = lax.broadcasted_iota(I32, zq.shape, 1)
    partner = jnp.where((lane % DA_QK) < half, pltpu.roll(zq, aw - half, axis=1), pltpu.roll(zq, half, axis=1))
    qa_ref[...] = ((zq * cosv + partner * sinv) * (DA_QK ** -0.5 * LOG2E)).astype(BF16)

    kgroups = aw // DA_QK
    zk = _dot_nt(wkt_ref[...], xn)
    pieces = []
    for g in range(kgroups):
        pieces += [zk[g * DA_QK + half:(g + 1) * DA_QK, :], zk[g * DA_QK:g * DA_QK + half, :]]
    zk_partner = jnp.concatenate(pieces, axis=0)
    kt = (zk * jnp.concatenate([cost_ref[...]] * kgroups, axis=0)
          + zk_partner * jnp.concatenate([sint_ref[...]] * kgroups, axis=0))

    va = mm(base + aw, base + 2 * aw)
    vab_ref[...] = va.astype(BF16)
    tm = va.shape[0]

    def store_kv(kt_out, ktb_out, v_out):
        kt_out[...] = kt
        ktb_out[...] = kt.astype(BF16)
        for h in range(DA_HEADS):
            v_out[pl.ds(h, tm, stride=DA_HEADS), :] = va[:, h * DA_V:(h + 1) * DA_V]

    @pl.when(is_prompt)
    def _():
        store_kv(ktp_ref, ktbp_ref, vp_ref)

    @pl.when(jnp.logical_not(is_prompt))
    def _():
        store_kv(kts_ref, ktbs_ref, vs_ref)

    base = base + 2 * aw
    d = g_ref.shape[-1]
    sga_ref[...] = _sigmoid(mm(base, base + d)).astype(BF16)
    sgb_ref[...] = _sigmoid(mm(base + d, base + 2 * d)).astype(BF16)


def _inproj(x_p, x_s, g_mix, w_main, w_kt, w_gr, w_gc, b_row, b_col, cos_t, sin_t, cos_tt, sin_tt, tm, seq):
    tp, d = x_p.shape
    ts = x_s.shape[0]
    npt, nst = tp // tm, ts // tm
    n_pos_tiles = seq // tm
    batch = tp // seq
    t_all = tp + ts
    w = ML_HEADS * ML_DK
    aw = DA_HEADS * 2 * DA_QK
    ncols = w_main.shape[1]

    def tok(i):
        return (i, 0)

    def const(i):
        return (0, 0)

    def p_idx(i):
        return (jnp.minimum(i, npt - 1), 0)

    def s_idx(i):
        return (jnp.maximum(i - npt, 0), 0)

    def pos_idx(i):
        return (jnp.where(i < npt, i % n_pos_tiles, n_pos_tiles), 0)

    def pos_idx_t(i):
        return (0, jnp.where(i < npt, i % n_pos_tiles, n_pos_tiles))

    def ktp_idx(i):
        ip = jnp.minimum(i, npt - 1)
        return (ip // n_pos_tiles, ip % n_pos_tiles)

    def kts_idx(i):
        return (0, jnp.maximum(i - npt, 0))

    bf = lambda n: jax.ShapeDtypeStruct((t_all, n), BF16)
    out_shape = (bf(w), bf(w), bf(w), bf(w),
                 jax.ShapeDtypeStruct((BF16_SUBLANES, t_all), F32),
                 jax.ShapeDtypeStruct((t_all, LANES), F32),
                 bf(aw), bf(aw), bf(d), bf(d),
                 jax.ShapeDtypeStruct((batch * aw, seq), F32), jax.ShapeDtypeStruct((batch * aw, seq), BF16),
                 jax.ShapeDtypeStruct((tp * DA_HEADS, DA_V), F32),
                 jax.ShapeDtypeStruct((aw, ts), F32), jax.ShapeDtypeStruct((aw, ts), BF16),
                 jax.ShapeDtypeStruct((ts * DA_HEADS, DA_V), F32))
    out_specs = (pl.BlockSpec((tm, w), tok), pl.BlockSpec((tm, w), tok), pl.BlockSpec((tm, w), tok),
                 pl.BlockSpec((tm, w), tok),
                 pl.BlockSpec((BF16_SUBLANES, tm), lambda i: (0, i)),
                 pl.BlockSpec((tm, LANES), tok),
                 pl.BlockSpec((tm, aw), tok), pl.BlockSpec((tm, aw), tok),
                 pl.BlockSpec((tm, d), tok), pl.BlockSpec((tm, d), tok),
                 pl.BlockSpec((aw, tm), ktp_idx), pl.BlockSpec((aw, tm), ktp_idx),
                 pl.BlockSpec((tm * DA_HEADS, DA_V), p_idx),
                 pl.BlockSpec((aw, tm), kts_idx), pl.BlockSpec((aw, tm), kts_idx),
                 pl.BlockSpec((tm * DA_HEADS, DA_V), s_idx))
    in_specs = [pl.BlockSpec((tm, d), p_idx), pl.BlockSpec((tm, d), s_idx),
                pl.BlockSpec((1, d), const),
                pl.BlockSpec((d, ncols), const, pipeline_mode=pl.Buffered(1)),
                pl.BlockSpec((aw, d), const, pipeline_mode=pl.Buffered(1)),
                pl.BlockSpec((BF16_SUBLANES, d), const),
                pl.BlockSpec((d, LANES), const),
                pl.BlockSpec((BF16_SUBLANES, 1), const),
                pl.BlockSpec((1, LANES), const),
                pl.BlockSpec((tm, LANES), pos_idx), pl.BlockSpec((tm, LANES), pos_idx),
                pl.BlockSpec((DA_QK, tm), pos_idx_t), pl.BlockSpec((DA_QK, tm), pos_idx_t)]
    return pl.pallas_call(
        functools.partial(_inproj_kernel, n_prompt_tiles=npt),
        grid=(npt + nst,), in_specs=in_specs, out_specs=out_specs, out_shape=out_shape,
        compiler_params=_params("arbitrary"), name="inproj",
    )(x_p, x_s, g_mix, w_main, w_kt, w_gr, w_gc, b_row, b_col, cos_t, sin_t, cos_tt, sin_tt)


def _mlstm_kernel(q_ref, k_ref, v_ref, og_ref, grow_ref, gcol_ref, c0_ref, n0_ref, m0_ref,
                  hg_ref, c_out, n_out, m_out, c_s, n_s, m_s, *, chunk):
    c = pl.program_id(1)
    nc = pl.num_programs(1)

    @pl.when(c == 0)
    def _():
        c_s[...] = c0_ref[...]
        n_s[...] = n0_ref[...]
        m_s[...] = m0_ref[...]

    L = chunk
    row = lax.broadcasted_iota(I32, (L, L), 0)
    col = lax.broadcasted_iota(I32, (L, L), 1)
    causal = col <= row
    tri = causal.astype(BF16)
    tri_t = (row <= col).astype(BF16)

    g_row = grow_ref[...]
    g_col = gcol_ref[...]
    cum_row = sum(_dot(p, tri_t) for p in _split3(g_row))
    cum_col = sum(_dot(tri, p) for p in _split3(g_col))

    for h in range(ML_HEADS):
        lo, hi = h * ML_DK, (h + 1) * ML_DK
        f = ML_HEADS + h
        b_rep = jnp.broadcast_to(cum_col[:, f:f + 1], (L, LANES))
        ig_rep = jnp.broadcast_to(g_col[:, h:h + 1], (L, LANES))
        b_row = cum_row[f:f + 1, :]
        ig_row = g_row[h:h + 1, :]
        m0 = m_s[h:h + 1, :]
        c0 = c_s[h]
        n0 = n_s[h:h + 1, :]
        qh = q_ref[:, lo:hi]
        kh = k_ref[:, lo:hi]
        vh = v_ref[:, h * ML_DV:(h + 1) * ML_DV]

        dmat = jnp.where(causal, _lanes(b_rep, L) - b_row + ig_row, -jnp.inf)
        inter = b_rep + m0
        mt = jnp.maximum(inter, jnp.max(dmat, axis=-1, keepdims=True))
        wts = jnp.exp(dmat - _lanes(mt, L)) * _dot_nt(qh, kh)
        decay0 = jnp.exp(inter - mt)
        num = _dot(wts.astype(BF16), vh) + decay0 * _dot_nt(qh, c0.astype(BF16))
        qn = jnp.sum(qh.astype(F32) * n0, axis=-1, keepdims=True)
        den = jnp.sum(wts, axis=-1, keepdims=True) + decay0 * qn
        hh = num / jnp.maximum(jnp.abs(den), jnp.exp(-mt))
        hg_ref[:, h * ML_DV:(h + 1) * ML_DV] = (hh * og_ref[:, h * ML_DV:(h + 1) * ML_DV].astype(F32)).astype(BF16)

        b_last = b_rep[L - 1:L, :]
        m_new = mt[L - 1:L, :]
        g_last = jnp.exp(b_last + m0 - m_new)
        ws = jnp.exp(b_last - b_rep + ig_rep - m_new)
        vw = (vh.astype(F32) * ws).astype(BF16)
        c_s[h] = g_last * c0 + _dot_tn(vw, kh)
        n_s[h:h + 1, :] = g_last * n0 + jnp.sum(kh.astype(F32) * ws, axis=0, keepdims=True)
        m_s[h:h + 1, :] = m_new

    @pl.when(c == nc - 1)
    def _():
        c_out[...] = c_s[...]
        n_out[...] = n_s[...]
        m_out[...] = m_s[...]


def _mlstm(q, k, v, og, grow3, gcol, c0, n0, m0, *, batch, chunk, row_block_offset):
    nchunks_total = grow3.shape[0]
    nc = nchunks_total // batch
    w = ML_HEADS * ML_DK
    wv = ML_HEADS * ML_DV

    def tok(b, c):
        return (row_block_offset + b * nc + c, 0)

    def tok0(b, c):
        return (b * nc + c, 0)

    def st4(b, c):
        return (b, 0, 0, 0)

    def st3(b, c):
        return (b, 0, 0)

    rows = batch * nc * chunk
    out_shape = (jax.ShapeDtypeStruct((rows, wv), BF16),
                 jax.ShapeDtypeStruct(c0.shape, F32),
                 jax.ShapeDtypeStruct(n0.shape, F32),
                 jax.ShapeDtypeStruct(m0.shape, F32))
    in_specs = [pl.BlockSpec((chunk, w), tok), pl.BlockSpec((chunk, w), tok), pl.BlockSpec((chunk, wv), tok),
                pl.BlockSpec((chunk, wv), tok),
                pl.BlockSpec((None, BF16_SUBLANES, chunk), lambda b, c: (b * nc + c, 0, 0)),
                pl.BlockSpec((chunk, LANES), tok),
                pl.BlockSpec((None, ML_HEADS, ML_DV, ML_DK), st4),
                pl.BlockSpec((None, ML_HEADS, ML_DK), st3),
                pl.BlockSpec((None, ML_HEADS, LANES), st3)]
    out_specs = (pl.BlockSpec((chunk, wv), tok0),
                 pl.BlockSpec((None, ML_HEADS, ML_DV, ML_DK), st4),
                 pl.BlockSpec((None, ML_HEADS, ML_DK), st3),
                 pl.BlockSpec((None, ML_HEADS, LANES), st3))
    scratch = [pltpu.VMEM((ML_HEADS, ML_DV, ML_DK), F32), pltpu.VMEM((ML_HEADS, ML_DK), F32),
               pltpu.VMEM((ML_HEADS, LANES), F32)]
    return pl.pallas_call(
        functools.partial(_mlstm_kernel, chunk=chunk),
        grid=(batch, nc), in_specs=in_specs, out_specs=out_specs, out_shape=out_shape,
        scratch_shapes=scratch, compiler_params=_params("arbitrary", "arbitrary"), name="mlstm",
    )(q, k, v, og, grow3, gcol, c0, n0, m0)


def _lambda(lamv_ref):
    lv = lamv_ref[...]
    s1 = jnp.sum(lv[0:1, :] * lv[1:2, :], axis=-1, keepdims=True)
    s2 = jnp.sum(lv[2:3, :] * lv[3:4, :], axis=-1, keepdims=True)
    return jnp.exp(s1) - jnp.exp(s2) + LAM_INIT


def _subln(o, gsub_ref):
    return _rms(o, SUBLN_EPS) * gsub_ref[...] * (1.0 - LAM_INIT)


def _lanes(a, width):
    if width <= LANES:
        return a[:, :width]
    return jnp.concatenate([a] * (width // LANES), axis=1)


def _online_softmax_step(s, v, m_s, l_s, acc_s):
    m_prev = m_s[...]
    m_new = jnp.maximum(m_prev, jnp.max(s, axis=-1, keepdims=True))
    alpha = jnp.exp2(m_prev - m_new)
    p = jnp.exp2(s - _lanes(m_new, s.shape[1]))
    l_s[...] = alpha * l_s[...] + jnp.sum(p, axis=-1, keepdims=True)
    acc_s[...] = _lanes(alpha, acc_s.shape[1]) * acc_s[...] + _dot(p.astype(BF16), v)
    m_s[...] = m_new


def _attn_kernel(qt_ref, kt_ref, q_ref, k_ref, v_ref, lamv_ref, gsub_ref, o_ref, q2_s, m_s, l_s, acc_s, *, blk):
    p = pl.program_id(1)
    qi = qt_ref[p]
    ki = kt_ref[p]
    hw = 2 * DA_QK

    @pl.when(ki == 0)
    def _():
        q = q_ref[...]
        lane = lax.broadcasted_iota(I32, q.shape, 1) % hw
        zero = jnp.zeros_like(q)
        qlo = jnp.where(lane < DA_QK, q, zero)
        qhi = jnp.where(lane >= DA_QK, q, zero)
        for h in range(DA_HEADS):
            q2_s[h, 0:blk, :] = qlo[:, h * hw:(h + 1) * hw]
            q2_s[h, blk:2 * blk, :] = qhi[:, h * hw:(h + 1) * hw]
        m_s[...] = jnp.full(m_s.shape, -jnp.inf, F32)
        l_s[...] = jnp.zeros(l_s.shape, F32)
        acc_s[...] = jnp.zeros(acc_s.shape, F32)

    def scores(h):
        return _dot(q2_s[h], k_ref[h * hw:(h + 1) * hw, :])

    @pl.when(ki < qi)
    def _():
        for h in range(DA_HEADS):
            _online_softmax_step(scores(h), v_ref[:, h * DA_V:(h + 1) * DA_V], m_s.at[h], l_s.at[h], acc_s.at[h])

    @pl.when(ki == qi)
    def _():
        r = lax.broadcasted_iota(I32, (2 * blk, blk), 0) % blk
        cidx = lax.broadcasted_iota(I32, (2 * blk, blk), 1)
        lam = _lambda(lamv_ref)
        for h in range(DA_HEADS):
            _online_softmax_step(jnp.where(cidx <= r, scores(h), -jnp.inf), v_ref[:, h * DA_V:(h + 1) * DA_V],
                                 m_s.at[h], l_s.at[h], acc_s.at[h])
            o2 = acc_s[h] / l_s[h]
            o = o2[0:blk, :] - lam * o2[blk:2 * blk, :]
            o_ref[:, h * DA_V:(h + 1) * DA_V] = _subln(o, gsub_ref).astype(BF16)


def _attn_prompt(qa, ktb, vab, lamv, gsub, *, batch, seq, blk):
    nq = seq // blk
    pairs = [(qi, ki) for qi in range(nq) for ki in range(qi + 1)]
    qt = jnp.asarray([p[0] for p in pairs], I32)
    kt = jnp.asarray([p[1] for p in pairs], I32)
    hw = 2 * DA_QK
    width = DA_HEADS * hw

    def q_idx(b, p, qt, kt):
        return (b * nq + qt[p], 0)

    def k_idx(b, p, qt, kt):
        return (b, kt[p])

    def v_idx(b, p, qt, kt):
        return (b * nq + kt[p], 0)

    def const(b, p, qt, kt):
        return (0, 0)

    grid_spec = pltpu.PrefetchScalarGridSpec(
        num_scalar_prefetch=2, grid=(batch, len(pairs)),
        in_specs=[pl.BlockSpec((blk, width), q_idx), pl.BlockSpec((width, blk), k_idx),
                  pl.BlockSpec((blk, DA_HEADS * DA_V), v_idx),
                  pl.BlockSpec(lamv.shape, const), pl.BlockSpec((1, DA_V), const)],
        out_specs=pl.BlockSpec((blk, DA_HEADS * DA_V), q_idx),
        scratch_shapes=[pltpu.VMEM((DA_HEADS, 2 * blk, hw), BF16), pltpu.VMEM((DA_HEADS, 2 * blk, LANES), F32),
                        pltpu.VMEM((DA_HEADS, 2 * blk, LANES), F32), pltpu.VMEM((DA_HEADS, 2 * blk, DA_V), F32)])
    return pl.pallas_call(
        functools.partial(_attn_kernel, blk=blk), grid_spec=grid_spec,
        out_shape=jax.ShapeDtypeStruct((batch * seq, DA_HEADS * DA_V), BF16),
        compiler_params=_params("arbitrary", "arbitrary"), name="attn_prompt",
    )(qt, kt, qa, ktb, vab, lamv, gsub)


def _dec_attn_kernel(pt_ref, q_ref, *refs, pages, n_new):
    k_refs = refs[:pages]
    v_refs = refs[pages:2 * pages]
    kn_ref, vn_ref, lamv_ref, gsub_ref, o_ref, m_s, l_s, acc_s = refs[2 * pages:]
    j = pl.program_id(1)
    nj = pl.num_programs(1)

    @pl.when(j == 0)
    def _():
        m_s[...] = jnp.full(m_s.shape, -jnp.inf, F32)
        l_s[...] = jnp.zeros(l_s.shape, F32)
        acc_s[...] = jnp.zeros(acc_s.shape, F32)

    page = k_refs[0].shape[1]
    half_rows = DA_HEADS * SUBLANES
    q = q_ref[...]

    def v_page(vr):
        return jnp.concatenate([vr[pl.ds(h, page, stride=DA_HEADS), :] for h in range(DA_HEADS)], axis=1).astype(BF16)

    s = jnp.concatenate([_dot(q, k_refs[p][...].astype(BF16)) for p in range(pages)], axis=1)
    m_prev = m_s[...]
    m_new = jnp.maximum(m_prev, jnp.max(s, axis=-1, keepdims=True))
    alpha = jnp.exp2(m_prev - m_new)
    pr = jnp.exp2(s - _lanes(m_new, s.shape[1]))
    pv = sum(_dot(pr[:, p * page:(p + 1) * page].astype(BF16), v_page(v_refs[p])) for p in range(pages))
    l_s[...] = alpha * l_s[...] + jnp.sum(pr, axis=-1, keepdims=True)
    acc_s[...] = _lanes(alpha, acc_s.shape[1]) * acc_s[...] + pv
    m_s[...] = m_new

    @pl.when(j == nj - 1)
    def _():
        sn = _dot(q, kn_ref[...])
        t = jnp.minimum(lax.broadcasted_iota(I32, sn.shape, 0) % SUBLANES, n_new - 1)
        cidx = lax.broadcasted_iota(I32, sn.shape, 1)
        _online_softmax_step(jnp.where(cidx <= t, sn, -jnp.inf), vn_ref[...], m_s, l_s, acc_s)
        o2 = acc_s[...] / _lanes(l_s[...], acc_s.shape[1])
        lam = _lambda(lamv_ref)
        outs = []
        for h in range(DA_HEADS):
            r0 = h * SUBLANES
            o0 = o2[r0:r0 + SUBLANES, h * DA_V:(h + 1) * DA_V]
            o1 = o2[half_rows + r0:half_rows + r0 + SUBLANES, h * DA_V:(h + 1) * DA_V]
            outs.append(_subln(o0 - lam * o1, gsub_ref))
        o_ref[...] = jnp.concatenate(outs, axis=1)


def _attn_sample(page_table, qbd, cache_kt, cache_v2, knt, vn, lamv, gsub, *, n_new, page):
    bs, npg = page_table.shape
    pages = min(PAGES_PER_STEP, npg)
    while npg % pages:
        pages -= 1
    rows, width = qbd.shape[1], qbd.shape[2]
    vrows = page * DA_HEADS

    def k_spec(p):
        return pl.BlockSpec((width, page), lambda b, j, pt: (pt[b, j * pages + p], 0))

    def v_spec(p):
        return pl.BlockSpec((vrows, DA_V), lambda b, j, pt: (pt[b, j * pages + p], 0))

    def seq3(b, j, pt):
        return (b, 0, 0)

    def const(b, j, pt):
        return (0, 0)

    in_specs = ([pl.BlockSpec((None, rows, width), seq3)]
                + [k_spec(p) for p in range(pages)] + [v_spec(p) for p in range(pages)]
                + [pl.BlockSpec((None, width, NEW_KV_PAD), seq3), pl.BlockSpec((None, NEW_KV_PAD, width), seq3),
                   pl.BlockSpec(lamv.shape, const), pl.BlockSpec((1, DA_V), const)])
    grid_spec = pltpu.PrefetchScalarGridSpec(
        num_scalar_prefetch=1, grid=(bs, npg // pages), in_specs=in_specs,
        out_specs=pl.BlockSpec((None, SUBLANES, width), seq3),
        scratch_shapes=[pltpu.VMEM((rows, LANES), F32), pltpu.VMEM((rows, LANES), F32), pltpu.VMEM((rows, width), F32)])
    return pl.pallas_call(
        functools.partial(_dec_attn_kernel, pages=pages, n_new=n_new), grid_spec=grid_spec,
        out_shape=jax.ShapeDtypeStruct((bs, SUBLANES, width), F32),
        compiler_params=_params("arbitrary", "arbitrary"), name="attn_sample",
    )(page_table, qbd, *([cache_kt] * pages), *([cache_v2] * pages), knt, vn, lamv, gsub)


def _merge_kernel(xp_ref, xs_ref, hgp_ref, hgs_ref, op_ref, os_ref, sga_ref, sgb_ref, wa_ref, wb_ref, wo_ref,
                  gffn_ref, wr_ref, br_ref, x1_ref, lg_ref, *, n_prompt_tiles):
    is_prompt = pl.program_id(0) < n_prompt_tiles
    x = jnp.where(is_prompt, xp_ref[...], xs_ref[...])
    hg = jnp.where(is_prompt, hgp_ref[...], hgs_ref[...])
    o = jnp.where(is_prompt, op_ref[...], os_ref[...])
    mixed = sga_ref[...].astype(F32) * _dot(hg, wa_ref[...]) + sgb_ref[...].astype(F32) * _dot(o, wb_ref[...])
    x1 = x + _dot(mixed.astype(BF16), wo_ref[...])
    _slab_store(x1_ref, x1, x1.shape[1] // LANES)
    xn = (_rms(x1, NORM_EPS) * gffn_ref[...]).astype(BF16)
    lg_ref[...] = _dot_nt(wr_ref[...], xn) + br_ref[...]


def _merge(x_p, x_s, hg_p, hg_s, o_p, o_s, sga, sgb, wa, wb, wo, g_ffn, wr, br, tm):
    tp, d = x_p.shape
    ts = x_s.shape[0]
    npt, nst = tp // tm, ts // tm
    t_all = tp + ts

    def tok(i):
        return (i, 0)

    def const(i):
        return (0, 0)

    def p_idx(i):
        return (jnp.minimum(i, npt - 1), 0)

    def s_idx(i):
        return (jnp.maximum(i - npt, 0), 0)

    wv, wo_in = hg_p.shape[1], o_p.shape[1]
    in_specs = [pl.BlockSpec((tm, d), p_idx), pl.BlockSpec((tm, d), s_idx),
                pl.BlockSpec((tm, wv), p_idx), pl.BlockSpec((tm, wv), s_idx),
                pl.BlockSpec((tm, wo_in), p_idx), pl.BlockSpec((tm, wo_in), s_idx),
                pl.BlockSpec((tm, d), tok), pl.BlockSpec((tm, d), tok),
                pl.BlockSpec(wa.shape, const), pl.BlockSpec(wb.shape, const), pl.BlockSpec(wo.shape, const),
                pl.BlockSpec((1, d), const), pl.BlockSpec(wr.shape, const), pl.BlockSpec((ROUTER_ROWS, 1), const)]
    ch = d // LANES
    out_shape = (jax.ShapeDtypeStruct((t_all * ch, LANES), F32), jax.ShapeDtypeStruct((ROUTER_ROWS, t_all), F32))
    out_specs = (pl.BlockSpec((tm * ch, LANES), tok), pl.BlockSpec((ROUTER_ROWS, tm), lambda i: (0, i)))
    return pl.pallas_call(
        functools.partial(_merge_kernel, n_prompt_tiles=npt),
        grid=(npt + nst,), in_specs=in_specs, out_specs=out_specs, out_shape=out_shape,
        compiler_params=_params("arbitrary"), name="merge",
    )(x_p, x_s, hg_p, hg_s, o_p, o_s, sga, sgb, wa, wb, wo, g_ffn, wr, br)


def _route_kernel(lg_ref, eid_ref, gw_ref):
    x = lg_ref[...]
    sub = lax.broadcasted_iota(I32, (SUBLANES, x.shape[1]), 0)
    lg = jnp.where(sub < MOE_GROUPS, x[0:SUBLANES, :], -jnp.inf)
    gmax = jnp.max(lg, axis=0, keepdims=True)
    g_star = jnp.min(jnp.where(lg == gmax, sub, SUBLANES), axis=0, keepdims=True)
    pg_top = 1.0 / jnp.sum(jnp.exp(lg - gmax), axis=0, keepdims=True)
    le = x[SUBLANES:2 * SUBLANES, :]
    for g in range(1, MOE_GROUPS):
        le = jnp.where(g_star == g, x[(g + 1) * SUBLANES:(g + 2) * SUBLANES, :], le)
    ex = jnp.exp(le - jnp.max(le, axis=0, keepdims=True))
    pe = ex / jnp.sum(ex, axis=0, keepdims=True)
    v1 = jnp.max(pe, axis=0, keepdims=True)
    i1 = jnp.min(jnp.where(pe == v1, sub, SUBLANES), axis=0, keepdims=True)
    rest = jnp.where(sub == i1, -jnp.inf, pe)
    v2 = jnp.max(rest, axis=0, keepdims=True)
    i2 = jnp.min(jnp.where(rest == v2, sub, SUBLANES), axis=0, keepdims=True)
    tot = v1 + v2
    e1 = g_star * MOE_PER_GROUP + i1
    e2 = g_star * MOE_PER_GROUP + i2
    w1 = pg_top * (v1 / tot)
    w2 = pg_top * (v2 / tot)
    eid_ref[...] = jnp.where(sub == 0, e1, jnp.where(sub == 1, e2, 0))
    gw_ref[...] = jnp.where(sub == 0, w1, jnp.where(sub == 1, w2, 0.0))


def _route(lg):
    rows, t_all = lg.shape
    tb = ROUTE_TILE
    while t_all % tb:
        tb //= 2
    spec = pl.BlockSpec((SUBLANES, tb), lambda i: (0, i))
    return pl.pallas_call(
        _route_kernel, grid=(t_all // tb,),
        in_specs=[pl.BlockSpec((rows, tb), lambda i: (0, i))], out_specs=(spec, spec),
        out_shape=(jax.ShapeDtypeStruct((SUBLANES, t_all), I32), jax.ShapeDtypeStruct((SUBLANES, t_all), F32)),
        compiler_params=_params("arbitrary"), name="route",
    )(lg)


def _expert_kernel(iblk_ref, iexp_ref, ilo_ref, ihi_ref, nit_ref, idx_hbm, x_hbm, w_ref, gffn_ref, wg_ref, wu_ref, wd_ref,
                   y_hbm, idx_s, xbuf, ybuf, wg_s, wu_s, wd_s, sem_i, sem_g, sem_s, *, rows, n_blocks, chunks):
    i = pl.program_id(0)
    valid = i < nit_ref[0]
    blk = iblk_ref[i]
    lo = ilo_ref[i]
    hi = ihi_ref[i]
    first = lo == 0
    last = hi == rows
    xs = blk % 2

    def idx_copy(b):
        return pltpu.make_async_copy(idx_hbm.at[b], idx_s.at[pl.ds((b % 3) * 2 * rows, 2 * rows)], sem_i.at[b % 3])

    def issue_gather(b):
        slot = b % 2
        base = (b % 3) * 2 * rows

        def body(r, carry):
            src = pl.multiple_of(idx_s[base + r] * chunks, chunks)
            pltpu.make_async_copy(x_hbm.at[pl.ds(src, chunks)], xbuf.at[slot, pl.ds(pl.multiple_of(r * chunks, chunks), chunks)],
                                  sem_g.at[slot]).start()
            return carry

        lax.fori_loop(0, rows, body, 0, unroll=DMA_ISSUE_UNROLL)

    def wait_gather(slot):
        pltpu.make_async_copy(x_hbm.at[pl.ds(0, rows * chunks)], xbuf.at[slot], sem_g.at[slot]).wait()

    def issue_scatter(b):
        slot = b % 2
        base = (b % 3) * 2 * rows + rows

        def body(r, carry):
            dst = pl.multiple_of(idx_s[base + r] * chunks, chunks)
            pltpu.make_async_copy(ybuf.at[slot, pl.ds(pl.multiple_of(r * chunks, chunks), chunks)], y_hbm.at[pl.ds(dst, chunks)],
                                  sem_s.at[slot]).start()
            return carry

        lax.fori_loop(0, rows, body, 0, unroll=DMA_ISSUE_UNROLL)

    def wait_scatter(slot):
        pltpu.make_async_copy(ybuf.at[slot], y_hbm.at[pl.ds(0, rows * chunks)], sem_s.at[slot]).wait()

    @pl.when(jnp.logical_and(valid, i == 0))
    def _():
        idx_copy(0).start()
        idx_copy(0).wait()
        if n_blocks > 1:
            idx_copy(1).start()
        issue_gather(0)

    @pl.when(jnp.logical_and(valid, first))
    def _():
        wait_gather(xs)

        @pl.when(blk + 1 < n_blocks)
        def _():
            idx_copy(blk + 1).wait()
            issue_gather(blk + 1)

        @pl.when(blk + 2 < n_blocks)
        def _():
            idx_copy(blk + 2).start()

        @pl.when(blk >= 2)
        def _():
            wait_scatter(xs)

    changed = jnp.logical_or(i == 0, iexp_ref[i] != iexp_ref[jnp.maximum(i - 1, 0)])

    @pl.when(jnp.logical_and(valid, changed))
    def _():
        wg_s[...] = wg_ref[...].astype(BF16)
        wu_s[...] = wu_ref[...].astype(BF16)
        wd_s[...] = wd_ref[...].astype(BF16)

    def compute():
        xn = (_rms(_slab_load(xbuf, rows, chunks, lead=xs), NORM_EPS) * gffn_ref[...]).astype(BF16)
        g = _dot(xn, wg_s[...])
        u = _dot(xn, wu_s[...])
        hmid = (g * _sigmoid(g) * u).astype(BF16)
        r = lax.broadcasted_iota(I32, (rows, 1), 0)
        wrow = jnp.where(jnp.logical_and(r >= lo, r < hi), w_ref[...], 0.0)
        return _dot(hmid, wd_s[...]) * wrow

    @pl.when(jnp.logical_and(valid, first))
    def _():
        _slab_store(ybuf, compute(), chunks, lead=xs)

    @pl.when(jnp.logical_and(valid, jnp.logical_not(first)))
    def _():
        _slab_store(ybuf, _slab_load(ybuf, rows, chunks, lead=xs) + compute(), chunks, lead=xs)

    @pl.when(jnp.logical_and(valid, last))
    def _():
        issue_scatter(blk)

    @pl.when(jnp.logical_and(valid, i == nit_ref[0] - 1))
    def _():
        wait_scatter(xs)

        @pl.when(blk >= 1)
        def _():
            wait_scatter(1 - xs)


def _experts(item_blk, item_exp, item_lo, item_hi, n_items, idx_rows, x1_slab, w_rows, g_ffn, w_gate, w_up, w_down, rows):
    n_blocks = idx_rows.shape[0]
    d, ff = w_gate.shape[-2], w_gate.shape[-1]
    chunks = d // LANES
    n_out = MOE_TOP_K * x1_slab.shape[0]
    n_max = item_blk.shape[0]

    def blk_idx(i, ib, ie, il, ih, nt):
        return (ib[i], 0)

    def const(i, ib, ie, il, ih, nt):
        return (0, 0)

    def w_idx(i, ib, ie, il, ih, nt):
        return (ie[i], 0, 0)

    grid_spec = pltpu.PrefetchScalarGridSpec(
        num_scalar_prefetch=5, grid=(n_max,),
        in_specs=[pl.BlockSpec(memory_space=pl.ANY), pl.BlockSpec(memory_space=pl.ANY),
                  pl.BlockSpec((rows, 1), blk_idx), pl.BlockSpec((1, d), const),
                  pl.BlockSpec((None, d, ff), w_idx), pl.BlockSpec((None, d, ff), w_idx),
                  pl.BlockSpec((None, ff, d), w_idx)],
        out_specs=pl.BlockSpec(memory_space=pl.ANY),
        scratch_shapes=[pltpu.SMEM((3 * 2 * rows,), I32),
                        pltpu.VMEM((2, rows * chunks, LANES), F32), pltpu.VMEM((2, rows * chunks, LANES), F32),
                        pltpu.VMEM((d, ff), BF16), pltpu.VMEM((d, ff), BF16), pltpu.VMEM((ff, d), BF16),
                        pltpu.SemaphoreType.DMA((3,)), pltpu.SemaphoreType.DMA((2,)), pltpu.SemaphoreType.DMA((2,))])
    return pl.pallas_call(
        functools.partial(_expert_kernel, rows=rows, n_blocks=n_blocks, chunks=chunks), grid_spec=grid_spec,
        out_shape=jax.ShapeDtypeStruct((n_out, LANES), F32),
        compiler_params=_params("arbitrary"), name="experts",
    )(item_blk, item_exp, item_lo, item_hi, n_items, idx_rows, x1_slab, w_rows, g_ffn, w_gate, w_up, w_down)


def _final_kernel(x1_ref, y0_ref, y1_ref, plep_ref, ples_ref, gple_ref, wpg_ref, wpp_ref, gfin_ref, yp_ref, ys_ref,
                  *, n_prompt_tiles):
    is_prompt = pl.program_id(0) < n_prompt_tiles
    tm, ch = plep_ref.shape[0], gple_ref.shape[1] // LANES
    x2 = _slab_load(x1_ref, tm, ch) + (_slab_load(y0_ref, tm, ch) + _slab_load(y1_ref, tm, ch))
    xn = (_rms(x2, NORM_EPS) * gple_ref[...]).astype(BF16)
    ple = jnp.where(is_prompt, plep_ref[...], ples_ref[...]).astype(BF16)
    x3 = x2 + _sigmoid(_dot(xn, wpg_ref[...])) * _dot(ple, wpp_ref[...])
    y = _rms(x3, NORM_EPS) * gfin_ref[...]

    @pl.when(is_prompt)
    def _():
        yp_ref[...] = y

    @pl.when(jnp.logical_not(is_prompt))
    def _():
        ys_ref[...] = y


def _final(x1, y_slots, ple_p, ple_s, g_ple, wpg, wpp, g_final, tm):
    d = g_ple.shape[1]
    ch = d // LANES
    tp, ts = ple_p.shape[0], ple_s.shape[0]
    npt, nst = tp // tm, ts // tm
    pd = ple_p.shape[1]
    nt = npt + nst

    def tok(i):
        return (i, 0)

    def tok1(i):
        return (nt + i, 0)

    def const(i):
        return (0, 0)

    def p_idx(i):
        return (jnp.minimum(i, npt - 1), 0)

    def s_idx(i):
        return (jnp.maximum(i - npt, 0), 0)

    in_specs = [pl.BlockSpec((tm * ch, LANES), tok), pl.BlockSpec((tm * ch, LANES), tok),
                pl.BlockSpec((tm * ch, LANES), tok1),
                pl.BlockSpec((tm, pd), p_idx), pl.BlockSpec((tm, pd), s_idx),
                pl.BlockSpec((1, d), const), pl.BlockSpec(wpg.shape, const), pl.BlockSpec(wpp.shape, const),
                pl.BlockSpec((1, d), const)]
    return pl.pallas_call(
        functools.partial(_final_kernel, n_prompt_tiles=npt),
        grid=(nt,), in_specs=in_specs,
        out_specs=(pl.BlockSpec((tm, d), p_idx), pl.BlockSpec((tm, d), s_idx)),
        out_shape=(jax.ShapeDtypeStruct((tp, d), F32), jax.ShapeDtypeStruct((ts, d), F32)),
        compiler_params=_params("arbitrary"), name="final",
    )(x1, y_slots, y_slots, ple_p, ple_s, g_ple, wpg, wpp, g_final)


def _rope_tables(pos):
    half = DA_QK // 2
    inv = ROPE_THETA ** (-jnp.arange(half, dtype=F32) / half)
    ang = pos.astype(F32)[:, None] * inv[None, :]
    cos, sin = jnp.cos(ang), jnp.sin(ang)
    reps = LANES // DA_QK
    cos_t = jnp.tile(jnp.concatenate([cos, cos], axis=1), (1, reps))
    sin_t = jnp.tile(jnp.concatenate([-sin, sin], axis=1), (1, reps))
    return cos_t, sin_t


def _tile(limit, *sizes):
    t = limit
    while any(s % t for s in sizes):
        t //= 2
    return t


def kernel(x_prompt, x_sample, cache_k, cache_v, state_mlstm_C, state_mlstm_n, state_mlstm_m, page_table, p_prompt, p_sample, g_mix, w_in, b_ml_i, b_ml_f, lam_q1, lam_k1, lam_q2, lam_k2, g_sub, w_br_a, w_br_b, w_out, g_ffn, w_rg, b_rg, w_re, b_re, w_e_gate, w_e_up, w_e_down, g_ple, w_ple_gate, w_ple_proj, g_final):
    depth = w_in.shape[0]
    assert depth == 1, "single-layer step"
    bp, sp, d = x_prompt.shape
    bs, ss, _ = x_sample.shape
    assert ss <= SUBLANES
    tp, ts = bp * sp, bs * ss
    t_all = tp + ts
    n_pages, page = page_table.shape[1], cache_k.shape[2]
    past_len = n_pages * page
    w = ML_HEADS * ML_DK
    aw = DA_HEADS * 2 * DA_QK
    li = 0

    wi = w_in[li]
    sizes = (w, w, ML_HEADS * ML_DV, ML_HEADS * ML_DV, ML_HEADS, ML_HEADS, aw, aw, DA_HEADS * DA_V, d, d)
    edges = [0]
    for n in sizes:
        edges.append(edges[-1] + n)
    assert edges[-1] == wi.shape[1]
    seg = [wi[:, edges[i]:edges[i + 1]] for i in range(11)]
    w_main = jnp.concatenate(seg[0:4] + [seg[6]] + seg[8:11], axis=1).astype(BF16)
    w_kt = seg[7].T.astype(BF16)
    w_gates = jnp.concatenate([seg[4], seg[5]], axis=1)
    w_gr = jnp.pad(w_gates.T, ((0, BF16_SUBLANES - 2 * ML_HEADS), (0, 0))).astype(BF16)
    w_gc = jnp.pad(w_gates, ((0, 0), (0, LANES - 2 * ML_HEADS))).astype(BF16)
    b_gates = jnp.concatenate([b_ml_i[li], b_ml_f[li]]).astype(F32)
    b_row = jnp.pad(b_gates, (0, BF16_SUBLANES - 2 * ML_HEADS))[:, None]
    b_col = jnp.pad(b_gates, (0, LANES - 2 * ML_HEADS))[None, :]
    lamv = jnp.stack([lam_q1[li], lam_k1[li], lam_q2[li], lam_k2[li]]).astype(F32)
    gsub = g_sub[li][None, :].astype(F32)
    w_router = jnp.zeros((ROUTER_ROWS, d), F32).at[0:MOE_GROUPS].set(w_rg[li].T).at[SUBLANES:SUBLANES + MOE_EXPERTS].set(w_re[li].T)
    b_router = jnp.zeros((ROUTER_ROWS,), F32).at[0:MOE_GROUPS].set(b_rg[li]).at[SUBLANES:SUBLANES + MOE_EXPERTS].set(b_re[li])

    tm = _tile(TOKEN_TILE, sp, ts)
    cos_p, sin_p = _rope_tables(jnp.arange(sp))
    cos_s, sin_s = _rope_tables(past_len + (jnp.arange(tm) % ss))
    cos_t = jnp.concatenate([cos_p, cos_s], axis=0)
    sin_t = jnp.concatenate([sin_p, sin_s], axis=0)
    cos_tt = cos_t[:, :DA_QK].T
    sin_tt = sin_t[:, :DA_QK].T
    xp2 = x_prompt.reshape(tp, d)
    xs2 = x_sample.reshape(ts, d)
    (q_ml, k_ml, v_ml, og, grow, gcol, qa, vab, sga, sgb, kt_p, ktb_p, v_p, kt_s, ktb_s, v_s) = _inproj(
        xp2, xs2, g_mix[li][None, :], w_main, w_kt, w_gr, w_gc, b_row, b_col, cos_t, sin_t, cos_tt, sin_tt, tm, sp)

    chunk = _tile(MLSTM_CHUNK, sp)
    ncp = sp // chunk
    grow3_p = grow[:, :tp].reshape(BF16_SUBLANES, bp * ncp, chunk).transpose(1, 0, 2)
    zc = jnp.zeros((bp, ML_HEADS, ML_DV, ML_DK), F32)
    zn = jnp.zeros((bp, ML_HEADS, ML_DK), F32)
    zm = jnp.zeros((bp, ML_HEADS, LANES), F32)
    hg_p, c_p, n_p, m_p = _mlstm(q_ml, k_ml, v_ml, og, grow3_p, gcol, zc, zn, zm,
                                 batch=bp, chunk=chunk, row_block_offset=0)

    padn = SAMPLE_PAD - ss

    def pad_seq(a):
        return jnp.pad(a[tp:].reshape(bs, ss, -1), ((0, 0), (0, padn), (0, 0))).reshape(bs * SAMPLE_PAD, -1)

    neutral = jnp.where(jnp.arange(LANES) < ML_HEADS, -1e30, 0.0).astype(F32)
    gcol_s = jnp.concatenate([gcol[tp:].reshape(bs, ss, LANES),
                              jnp.broadcast_to(neutral, (bs, padn, LANES))], axis=1).reshape(bs * SAMPLE_PAD, LANES)
    grow3_s = jnp.concatenate([grow[:, tp:].reshape(BF16_SUBLANES, bs, ss).transpose(1, 0, 2),
                               jnp.broadcast_to(neutral[:BF16_SUBLANES, None], (bs, BF16_SUBLANES, padn))], axis=2)
    m0_s = jnp.broadcast_to(state_mlstm_m[li].astype(F32)[:, :, None], (bs, ML_HEADS, LANES))
    hg_s_pad, c_s, n_s, m_s = _mlstm(pad_seq(q_ml), pad_seq(k_ml), pad_seq(v_ml), pad_seq(og), grow3_s, gcol_s,
                                     state_mlstm_C[li].astype(F32), state_mlstm_n[li].astype(F32), m0_s,
                                     batch=bs, chunk=SAMPLE_PAD, row_block_offset=0)
    hg_s = hg_s_pad.reshape(bs, SAMPLE_PAD, -1)[:, :ss].reshape(ts, -1)

    blk = _tile(ATTN_BLOCK, sp)
    o_p = _attn_prompt(qa, ktb_p, vab, lamv, gsub, batch=bp, seq=sp, blk=blk)

    q_s = jnp.pad(qa[tp:].reshape(bs, ss, DA_HEADS, 2, DA_QK), ((0, 0), (0, SUBLANES - ss), (0, 0), (0, 0), (0, 0)))
    q_cht = q_s.transpose(0, 3, 2, 1, 4)
    same = jnp.logical_and(
        (jnp.arange(DA_HEADS)[:, None] == jnp.arange(DA_HEADS)[None, :])[None, None, :, None, :, None, None],
        (jnp.arange(2)[:, None] == jnp.arange(2)[None, :])[None, :, None, None, None, :, None])
    qbd = jnp.where(same, q_cht[:, :, :, :, None, None, :], jnp.zeros((), BF16)).reshape(bs, 2 * DA_HEADS * SUBLANES, aw)
    knt = jnp.pad(ktb_s.reshape(aw, bs, ss).transpose(1, 0, 2), ((0, 0), (0, 0), (0, NEW_KV_PAD - ss)))
    vn = jnp.pad(vab[tp:].reshape(bs, ss, aw), ((0, 0), (0, NEW_KV_PAD - ss), (0, 0)))
    ckt = cache_k[li].transpose(0, 2, 3, 4, 1).reshape(-1, page)
    cv2 = cache_v[li].reshape(-1, DA_V)
    o_s = _attn_sample(page_table, qbd, ckt, cv2, knt, vn, lamv, gsub, n_new=ss, page=page)
    o_s = o_s[:, :ss].reshape(ts, aw).astype(BF16)

    x1, lg = _merge(xp2, xs2, hg_p, hg_s, o_p, o_s, sga, sgb,
                    w_br_a[li].astype(BF16), w_br_b[li].astype(BF16), w_out[li].astype(BF16),
                    g_ffn[li][None, :], w_router.astype(BF16), b_router[:, None], tm)

    eid8, gw8 = _route(lg)
    n_slots = t_all * MOE_TOP_K
    rows = _tile(EXPERT_ROWS, n_slots)
    nb = n_slots // rows
    flat_e = eid8[:MOE_TOP_K].T.reshape(-1)
    flat_w = gw8[:MOE_TOP_K].T.reshape(-1)
    slot_bits = max(1, (n_slots - 1).bit_length())
    assert MOE_EXPERTS << slot_bits < 2 ** 31
    key = (flat_e << slot_bits) | jnp.arange(n_slots, dtype=I32)
    sorted_key, sorted_w = lax.sort((key, flat_w), num_keys=1)
    order = sorted_key & ((1 << slot_bits) - 1)
    tok_rows = (order // MOE_TOP_K).reshape(nb, rows)
    dst_rows = ((order % MOE_TOP_K) * t_all + order // MOE_TOP_K).reshape(nb, rows)
    idx_rows = jnp.concatenate([tok_rows, dst_rows], axis=1)
    counts = jnp.sum(flat_e[None, :] == jnp.arange(MOE_EXPERTS, dtype=I32)[:, None], axis=1, dtype=I32)
    ends = jnp.cumsum(counts)
    starts = ends - counts
    first_blk = starts // rows
    n_e = jnp.where(counts > 0, (ends - 1) // rows - first_blk + 1, 0)
    item_end = jnp.cumsum(n_e)
    item_start = item_end - n_e
    n_items = item_end[-1:]
    n_max = nb + MOE_EXPERTS - 1
    it = jnp.minimum(jnp.arange(n_max, dtype=I32), n_items[0] - 1)
    item_exp = jnp.sum(item_end[None, :] <= it[:, None], axis=1, dtype=I32)
    item_blk = first_blk[item_exp] + it - item_start[item_exp]
    item_lo = jnp.maximum(starts[item_exp] - item_blk * rows, 0)
    item_hi = jnp.minimum(ends[item_exp] - item_blk * rows, rows)
    y_slots = _experts(item_blk, item_exp, item_lo, item_hi, n_items, idx_rows, x1, sorted_w[:, None],
                       g_ffn[li][None, :], w_e_gate[li], w_e_up[li], w_e_down[li], rows)

    pd = p_prompt.shape[-1]
    y_p, y_s = _final(x1, y_slots, p_prompt[li].reshape(tp, pd), p_sample[li].reshape(ts, pd), g_ple[li][None, :],
                      w_ple_gate[li].astype(BF16), w_ple_proj[li].astype(BF16), g_final[None, :], tm)

    return (y_p.reshape(bp, sp, d), y_s.reshape(bs, ss, d),
            kt_p.reshape(1, bp, DA_HEADS, 2, DA_QK, sp).transpose(0, 1, 5, 2, 3, 4), v_p.reshape(1, bp, sp, DA_HEADS, DA_V),
            c_p[None], n_p[None], m_p[None, :, :, 0],
            kt_s.reshape(1, DA_HEADS, 2, DA_QK, bs, ss).transpose(0, 4, 5, 1, 2, 3), v_s.reshape(1, bs, ss, DA_HEADS, DA_V),
            c_s[None], n_s[None], m_s[None, :, :, 0])
```

```python
import functools
import math

import jax
import jax.numpy as jnp
from jax import lax
from jax.experimental import pallas as pl
from jax.experimental.pallas import tpu as pltpu

F32 = jnp.float32
BF16 = jnp.bfloat16
I32 = jnp.int32

ML_HEADS = 4
ML_DK = 128
ML_DV = 128
DA_HEADS = 4
DA_QK = 64
DA_V = 128
ROPE_THETA = 10000.0
MOE_GROUPS = 4
MOE_PER_GROUP = 8
MOE_EXPERTS = MOE_GROUPS * MOE_PER_GROUP
MOE_TOP_K = 2
NORM_EPS = 1e-6
SUBLN_EPS = 1e-5
LAYER_INDEX = 0
LAM_INIT = 0.8 - 0.6 * math.exp(-0.3 * LAYER_INDEX)
LOG2E = math.log2(math.e)

LANES = 128
SUBLANES = 8
BF16_SUBLANES = 16
VMEM_LIMIT_BYTES = 56 * 1024 * 1024

TOKEN_TILE = 512
MLSTM_CHUNK = 256
ATTN_BLOCK = 512
PAGES_PER_STEP = 8
EXPERT_ROWS = 512
ROUTE_TILE = 512
ROUTER_ROWS = 48
SAMPLE_PAD = 16
NEW_KV_PAD = 128
DMA_ISSUE_UNROLL = 8

NT_DIMS = (((1,), (1,)), ((), ()))
TN_DIMS = (((0,), (0,)), ((), ()))


def _params(*sem):
    return pltpu.CompilerParams(dimension_semantics=sem, vmem_limit_bytes=VMEM_LIMIT_BYTES)


def _sigmoid(x):
    return 1.0 / (1.0 + jnp.exp(-x))


def _log_sigmoid(x):
    return jnp.minimum(x, 0.0) - jnp.log1p(jnp.exp(-jnp.abs(x)))


def _rms(x, eps):
    return x * lax.rsqrt(jnp.mean(x * x, axis=-1, keepdims=True) + eps)


def _dot(a, b):
    return jnp.dot(a, b, preferred_element_type=F32)


def _dot_nt(a, b):
    return lax.dot_general(a, b, NT_DIMS, preferred_element_type=F32)


def _dot_tn(a, b):
    return lax.dot_general(a, b, TN_DIMS, preferred_element_type=F32)


def _slab_load(ref, rows, chunks, lead=None):
    def piece(c):
        idx = (pl.ds(c, rows, stride=chunks), slice(None))
        return ref[idx] if lead is None else ref[(lead,) + idx]
    return jnp.concatenate([piece(c) for c in range(chunks)], axis=1)


def _slab_store(ref, val, chunks, lead=None):
    rows = val.shape[0]
    for c in range(chunks):
        idx = (pl.ds(c, rows, stride=chunks), slice(None))
        ref[idx if lead is None else (lead,) + idx] = val[:, c * LANES:(c + 1) * LANES]


def _split3(a):
    hi = a.astype(BF16)
    r1 = a - hi.astype(F32)
    mid = r1.astype(BF16)
    lo = (r1 - mid.astype(F32)).astype(BF16)
    return hi, mid, lo


def _inproj_kernel(xp_ref, xs_ref, g_ref, wm_ref, wkt_ref, wgr_ref, wgc_ref, br_ref, bc_ref, cos_ref, sin_ref,
                   cost_ref, sint_ref,
                   q_ref, k_ref, v_ref, og_ref, grow_ref, gcol_ref, qa_ref, vab_ref, sga_ref, sgb_ref,
                   ktp_ref, ktbp_ref, vp_ref, kts_ref, ktbs_ref, vs_ref, *, n_prompt_tiles):
    i = pl.program_id(0)
    is_prompt = i < n_prompt_tiles
    x = jnp.where(is_prompt, xp_ref[...], xs_ref[...])
    xn = (_rms(x, NORM_EPS) * g_ref[...]).astype(BF16)

    def mm(lo, hi):
        return _dot(xn, wm_ref[:, lo:hi])

    w = ML_HEADS * ML_DK
    q_ref[...] = mm(0, w).astype(BF16)
    k_ref[...] = (mm(w, 2 * w) * (ML_DK ** -0.5)).astype(BF16)
    v_ref[...] = mm(2 * w, 3 * w).astype(BF16)
    og_ref[...] = _sigmoid(mm(3 * w, 4 * w)).astype(BF16)

    gr = _dot_nt(wgr_ref[...], xn) + br_ref[...]
    rr = lax.broadcasted_iota(I32, gr.shape, 0)
    grow_ref[...] = jnp.where(rr >= ML_HEADS, _log_sigmoid(gr), gr)
    gc = _dot(xn, wgc_ref[...]) + bc_ref[...]
    cc = lax.broadcasted_iota(I32, gc.shape, 1)
    gcol_ref[...] = jnp.where(cc >= ML_HEADS, _log_sigmoid(gc), gc)

    aw = DA_HEADS * 2 * DA_QK
    reps = aw // LANES
    cosv = jnp.concatenate([cos_ref[...]] * reps, axis=1)
    sinv = jnp.concatenate([sin_ref[...]] * reps, axis=1)
    half = DA_QK // 2
    base = 4 * w
    zq = mm(base, base + aw)
    lane = lax.broadcasted_iota(I32, zq.shape, 1)
    partner = jnp.where((lane % DA_QK) < half, pltpu.roll(zq, aw - half, axis=1), pltpu.roll(zq, half, axis=1))
    qa_ref[...] = ((zq * cosv + partner * sinv) * (DA_QK ** -0.5 * LOG2E)).astype(BF16)

    kgroups = aw // DA_QK
    zk = _dot_nt(wkt_ref[...], xn)
    pieces = []
    for g in range(kgroups):
        pieces += [zk[g * DA_QK + half:(g + 1) * DA_QK, :], zk[g * DA_QK:g * DA_QK + half, :]]
    zk_partner = jnp.concatenate(pieces, axis=0)
    kt = (zk * jnp.concatenate([cost_ref[...]] * kgroups, axis=0)
          + zk_partner * jnp.concatenate([sint_ref[...]] * kgroups, axis=0))

    va = mm(base + aw, base + 2 * aw)
    vab_ref[...] = va.astype(BF16)
    tm = va.shape[0]

    def store_kv(kt_out, ktb_out, v_out):
        kt_out[...] = kt
        ktb_out[...] = kt.astype(BF16)
        for h in range(DA_HEADS):
            v_out[pl.ds(h, tm, stride=DA_HEADS), :] = va[:, h * DA_V:(h + 1) * DA_V]

    @pl.when(is_prompt)
    def _():
        store_kv(ktp_ref, ktbp_ref, vp_ref)

    @pl.when(jnp.logical_not(is_prompt))
    def _():
        store_kv(kts_ref, ktbs_ref, vs_ref)

    base = base + 2 * aw
    d = g_ref.shape[-1]
    sga_ref[...] = _sigmoid(mm(base, base + d)).astype(BF16)
    sgb_ref[...] = _sigmoid(mm(base + d, base + 2 * d)).astype(BF16)


def _inproj(x_p, x_s, g_mix, w_main, w_kt, w_gr, w_gc, b_row, b_col, cos_t, sin_t, cos_tt, sin_tt, tm, seq):
    tp, d = x_p.shape
    ts = x_s.shape[0]
    npt, nst = tp // tm, ts // tm
    n_pos_tiles = seq // tm
    batch = tp // seq
    t_all = tp + ts
    w = ML_HEADS * ML_DK
    aw = DA_HEADS * 2 * DA_QK
    ncols = w_main.shape[1]

    def tok(i):
        return (i, 0)

    def const(i):
        return (0, 0)

    def p_idx(i):
        return (jnp.minimum(i, npt - 1), 0)

    def s_idx(i):
        return (jnp.maximum(i - npt, 0), 0)

    def pos_idx(i):
        return (jnp.where(i < npt, i % n_pos_tiles, n_pos_tiles), 0)

    def pos_idx_t(i):
        return (0, jnp.where(i < npt, i % n_pos_tiles, n_pos_tiles))

    def ktp_idx(i):
        ip = jnp.minimum(i, npt - 1)
        return (ip // n_pos_tiles, ip % n_pos_tiles)

    def kts_idx(i):
        return (0, jnp.maximum(i - npt, 0))

    bf = lambda n: jax.ShapeDtypeStruct((t_all, n), BF16)
    out_shape = (bf(w), bf(w), bf(w), bf(w),
                 jax.ShapeDtypeStruct((BF16_SUBLANES, t_all), F32),
                 jax.ShapeDtypeStruct((t_all, LANES), F32),
                 bf(aw), bf(aw), bf(d), bf(d),
                 jax.ShapeDtypeStruct((batch * aw, seq), F32), jax.ShapeDtypeStruct((batch * aw, seq), BF16),
                 jax.ShapeDtypeStruct((tp * DA_HEADS, DA_V), F32),
                 jax.ShapeDtypeStruct((aw, ts), F32), jax.ShapeDtypeStruct((aw, ts), BF16),
                 jax.ShapeDtypeStruct((ts * DA_HEADS, DA_V), F32))
    out_specs = (pl.BlockSpec((tm, w), tok), pl.BlockSpec((tm, w), tok), pl.BlockSpec((tm, w), tok),
                 pl.BlockSpec((tm, w), tok),
                 pl.BlockSpec((BF16_SUBLANES, tm), lambda i: (0, i)),
                 pl.BlockSpec((tm, LANES), tok),
                 pl.BlockSpec((tm, aw), tok), pl.BlockSpec((tm, aw), tok),
                 pl.BlockSpec((tm, d), tok), pl.BlockSpec((tm, d), tok),
                 pl.BlockSpec((aw, tm), ktp_idx), pl.BlockSpec((aw, tm), ktp_idx),
                 pl.BlockSpec((tm * DA_HEADS, DA_V), p_idx),
                 pl.BlockSpec((aw, tm), kts_idx), pl.BlockSpec((aw, tm), kts_idx),
                 pl.BlockSpec((tm * DA_HEADS, DA_V), s_idx))
    in_specs = [pl.BlockSpec((tm, d), p_idx), pl.BlockSpec((tm, d), s_idx),
                pl.BlockSpec((1, d), const),
                pl.BlockSpec((d, ncols), const, pipeline_mode=pl.Buffered(1)),
                pl.BlockSpec((aw, d), const, pipeline_mode=pl.Buffered(1)),
                pl.BlockSpec((BF16_SUBLANES, d), const),
                pl.BlockSpec((d, LANES), const),
                pl.BlockSpec((BF16_SUBLANES, 1), const),
                pl.BlockSpec((1, LANES), const),
                pl.BlockSpec((tm, LANES), pos_idx), pl.BlockSpec((tm, LANES), pos_idx),
                pl.BlockSpec((DA_QK, tm), pos_idx_t), pl.BlockSpec((DA_QK, tm), pos_idx_t)]
    return pl.pallas_call(
        functools.partial(_inproj_kernel, n_prompt_tiles=npt),
        grid=(npt + nst,), in_specs=in_specs, out_specs=out_specs, out_shape=out_shape,
        compiler_params=_params("arbitrary"), name="inproj",
    )(x_p, x_s, g_mix, w_main, w_kt, w_gr, w_gc, b_row, b_col, cos_t, sin_t, cos_tt, sin_tt)


def _mlstm_kernel(q_ref, k_ref, v_ref, og_ref, grow_ref, gcol_ref, c0_ref, n0_ref, m0_ref,
                  hg_ref, c_out, n_out, m_out, c_s, n_s, m_s, *, chunk):
    c = pl.program_id(1)
    nc = pl.num_programs(1)

    @pl.when(c == 0)
    def _():
        c_s[...] = c0_ref[...]
        n_s[...] = n0_ref[...]
        m_s[...] = m0_ref[...]

    L = chunk
    row = lax.broadcasted_iota(I32, (L, L), 0)
    col = lax.broadcasted_iota(I32, (L, L), 1)
    causal = col <= row
    tri = causal.astype(BF16)
    tri_t = (row <= col).astype(BF16)

    g_row = grow_ref[...]
    g_col = gcol_ref[...]
    cum_row = sum(_dot(p, tri_t) for p in _split3(g_row))
    cum_col = sum(_dot(tri, p) for p in _split3(g_col))

    for h in range(ML_HEADS):
        lo, hi = h * ML_DK, (h + 1) * ML_DK
        f = ML_HEADS + h
        b_rep = jnp.broadcast_to(cum_col[:, f:f + 1], (L, LANES))
        ig_rep = jnp.broadcast_to(g_col[:, h:h + 1], (L, LANES))
        b_row = cum_row[f:f + 1, :]
        ig_row = g_row[h:h + 1, :]
        m0 = m_s[h:h + 1, :]
        c0 = c_s[h]
        n0 = n_s[h:h + 1, :]
        qh = q_ref[:, lo:hi]
        kh = k_ref[:, lo:hi]
        vh = v_ref[:, h * ML_DV:(h + 1) * ML_DV]

        dmat = jnp.where(causal, _lanes(b_rep, L) - b_row + ig_row, -jnp.inf)
        inter = b_rep + m0
        mt = jnp.maximum(inter, jnp.max(dmat, axis=-1, keepdims=True))
        wts = jnp.exp(dmat - _lanes(mt, L)) * _dot_nt(qh, kh)
        decay0 = jnp.exp(inter - mt)
        num = _dot(wts.astype(BF16), vh) + decay0 * _dot_nt(qh, c0.astype(BF16))
        qn = jnp.sum(qh.astype(F32) * n0, axis=-1, keepdims=True)
        den = jnp.sum(wts, axis=-1, keepdims=True) + decay0 * qn
        hh = num / jnp.maximum(jnp.abs(den), jnp.exp(-mt))
        hg_ref[:, h * ML_DV:(h + 1) * ML_DV] = (hh * og_ref[:, h * ML_DV:(h + 1) * ML_DV].astype(F32)).astype(BF16)

        b_last = b_rep[L - 1:L, :]
        m_new = mt[L - 1:L, :]
        g_last = jnp.exp(b_last + m0 - m_new)
        ws = jnp.exp(b_last - b_rep + ig_rep - m_new)
        vw = (vh.astype(F32) * ws).astype(BF16)
        c_s[h] = g_last * c0 + _dot_tn(vw, kh)
        n_s[h:h + 1, :] = g_last * n0 + jnp.sum(kh.astype(F32) * ws, axis=0, keepdims=True)
        m_s[h:h + 1, :] = m_new

    @pl.when(c == nc - 1)
    def _():
        c_out[...] = c_s[...]
        n_out[...] = n_s[...]
        m_out[...] = m_s[...]


def _mlstm(q, k, v, og, grow3, gcol, c0, n0, m0, *, batch, chunk, row_block_offset):
    nchunks_total = grow3.shape[0]
    nc = nchunks_total // batch
    w = ML_HEADS * ML_DK
    wv = ML_HEADS * ML_DV

    def tok(b, c):
        return (row_block_offset + b * nc + c, 0)

    def tok0(b, c):
        return (b * nc + c, 0)

    def st4(b, c):
        return (b, 0, 0, 0)

    def st3(b, c):
        return (b, 0, 0)

    rows = batch * nc * chunk
    out_shape = (jax.ShapeDtypeStruct((rows, wv), BF16),
                 jax.ShapeDtypeStruct(c0.shape, F32),
                 jax.ShapeDtypeStruct(n0.shape, F32),
                 jax.ShapeDtypeStruct(m0.shape, F32))
    in_specs = [pl.BlockSpec((chunk, w), tok), pl.BlockSpec((chunk, w), tok), pl.BlockSpec((chunk, wv), tok),
                pl.BlockSpec((chunk, wv), tok),
                pl.BlockSpec((None, BF16_SUBLANES, chunk), lambda b, c: (b * nc + c, 0, 0)),
                pl.BlockSpec((chunk, LANES), tok),
                pl.BlockSpec((None, ML_HEADS, ML_DV, ML_DK), st4),
                pl.BlockSpec((None, ML_HEADS, ML_DK), st3),
                pl.BlockSpec((None, ML_HEADS, LANES), st3)]
    out_specs = (pl.BlockSpec((chunk, wv), tok0),
                 pl.BlockSpec((None, ML_HEADS, ML_DV, ML_DK), st4),
                 pl.BlockSpec((None, ML_HEADS, ML_DK), st3),
                 pl.BlockSpec((None, ML_HEADS, LANES), st3))
    scratch = [pltpu.VMEM((ML_HEADS, ML_DV, ML_DK), F32), pltpu.VMEM((ML_HEADS, ML_DK), F32),
               pltpu.VMEM((ML_HEADS, LANES), F32)]
    return pl.pallas_call(
        functools.partial(_mlstm_kernel, chunk=chunk),
        grid=(batch, nc), in_specs=in_specs, out_specs=out_specs, out_shape=out_shape,
        scratch_shapes=scratch, compiler_params=_params("arbitrary", "arbitrary"), name="mlstm",
    )(q, k, v, og, grow3, gcol, c0, n0, m0)


def _lambda(lamv_ref):
    lv = lamv_ref[...]
    s1 = jnp.sum(lv[0:1, :] * lv[1:2, :], axis=-1, keepdims=True)
    s2 = jnp.sum(lv[2:3, :] * lv[3:4, :], axis=-1, keepdims=True)
    return jnp.exp(s1) - jnp.exp(s2) + LAM_INIT


def _subln(o, gsub_ref):
    return _rms(o, SUBLN_EPS) * gsub_ref[...] * (1.0 - LAM_INIT)


def _lanes(a, width):
    if width <= LANES:
        return a[:, :width]
    return jnp.concatenate([a] * (width // LANES), axis=1)


def _online_softmax_step(s, v, m_s, l_s, acc_s):
    m_prev = m_s[...]
    m_new = jnp.maximum(m_prev, jnp.max(s, axis=-1, keepdims=True))
    alpha = jnp.exp2(m_prev - m_new)
    p = jnp.exp2(s - _lanes(m_new, s.shape[1]))
    l_s[...] = alpha * l_s[...] + jnp.sum(p, axis=-1, keepdims=True)
    acc_s[...] = _lanes(alpha, acc_s.shape[1]) * acc_s[...] + _dot(p.astype(BF16), v)
    m_s[...] = m_new


def _attn_kernel(qt_ref, kt_ref, q_ref, k_ref, v_ref, lamv_ref, gsub_ref, o_ref, q2_s, m_s, l_s, acc_s, *, blk):
    p = pl.program_id(1)
    qi = qt_ref[p]
    ki = kt_ref[p]
    hw = 2 * DA_QK

    @pl.when(ki == 0)
    def _():
        q = q_ref[...]
        lane = lax.broadcasted_iota(I32, q.shape, 1) % hw
        zero = jnp.zeros_like(q)
        qlo = jnp.where(lane < DA_QK, q, zero)
        qhi = jnp.where(lane >= DA_QK, q, zero)
        for h in range(DA_HEADS):
            q2_s[h, 0:blk, :] = qlo[:, h * hw:(h + 1) * hw]
            q2_s[h, blk:2 * blk, :] = qhi[:, h * hw:(h + 1) * hw]
        m_s[...] = jnp.full(m_s.shape, -jnp.inf, F32)
        l_s[...] = jnp.zeros(l_s.shape, F32)
        acc_s[...] = jnp.zeros(acc_s.shape, F32)

    def scores(h):
        return _dot(q2_s[h], k_ref[h * hw:(h + 1) * hw, :])

    @pl.when(ki < qi)
    def _():
        for h in range(DA_HEADS):
            _online_softmax_step(scores(h), v_ref[:, h * DA_V:(h + 1) * DA_V], m_s.at[h], l_s.at[h], acc_s.at[h])

    @pl.when(ki == qi)
    def _():
        r = lax.broadcasted_iota(I32, (2 * blk, blk), 0) % blk
        cidx = lax.broadcasted_iota(I32, (2 * blk, blk), 1)
        lam = _lambda(lamv_ref)
        for h in range(DA_HEADS):
            _online_softmax_step(jnp.where(cidx <= r, scores(h), -jnp.inf), v_ref[:, h * DA_V:(h + 1) * DA_V],
                                 m_s.at[h], l_s.at[h], acc_s.at[h])
            o2 = acc_s[h] / l_s[h]
            o = o2[0:blk, :] - lam * o2[blk:2 * blk, :]
            o_ref[:, h * DA_V:(h + 1) * DA_V] = _subln(o, gsub_ref).astype(BF16)


def _attn_prompt(qa, ktb, vab, lamv, gsub, *, batch, seq, blk):
    nq = seq // blk
    pairs = [(qi, ki) for qi in range(nq) for ki in range(qi + 1)]
    qt = jnp.asarray([p[0] for p in pairs], I32)
    kt = jnp.asarray([p[1] for p in pairs], I32)
    hw = 2 * DA_QK
    width = DA_HEADS * hw

    def q_idx(b, p, qt, kt):
        return (b * nq + qt[p], 0)

    def k_idx(b, p, qt, kt):
        return (b, kt[p])

    def v_idx(b, p, qt, kt):
        return (b * nq + kt[p], 0)

    def const(b, p, qt, kt):
        return (0, 0)

    grid_spec = pltpu.PrefetchScalarGridSpec(
        num_scalar_prefetch=2, grid=(batch, len(pairs)),
        in_specs=[pl.BlockSpec((blk, width), q_idx), pl.BlockSpec((width, blk), k_idx),
                  pl.BlockSpec((blk, DA_HEADS * DA_V), v_idx),
                  pl.BlockSpec(lamv.shape, const), pl.BlockSpec((1, DA_V), const)],
        out_specs=pl.BlockSpec((blk, DA_HEADS * DA_V), q_idx),
        scratch_shapes=[pltpu.VMEM((DA_HEADS, 2 * blk, hw), BF16), pltpu.VMEM((DA_HEADS, 2 * blk, LANES), F32),
                        pltpu.VMEM((DA_HEADS, 2 * blk, LANES), F32), pltpu.VMEM((DA_HEADS, 2 * blk, DA_V), F32)])
    return pl.pallas_call(
        functools.partial(_attn_kernel, blk=blk), grid_spec=grid_spec,
        out_shape=jax.ShapeDtypeStruct((batch * seq, DA_HEADS * DA_V), BF16),
        compiler_params=_params("arbitrary", "arbitrary"), name="attn_prompt",
    )(qt, kt, qa, ktb, vab, lamv, gsub)


def _dec_attn_kernel(pt_ref, q_ref, k_hbm, v_hbm, kn_ref, vn_ref, lamv_ref, gsub_ref, o_ref,
                     kbuf, vbuf, m_s, l_s, acc_s, sem_k, sem_v, *, pages, page, n_new):
    b = pl.program_id(0)
    j = pl.program_id(1)
    nj = pl.num_programs(1)
    t = b * nj + j
    n_steps = pl.num_programs(0) * nj
    krows = DA_HEADS * 2 * DA_QK
    vrows = page * DA_HEADS

    def k_copy(step, p):
        pg = pt_ref[step // nj, (step % nj) * pages + p]
        return pltpu.make_async_copy(k_hbm.at[pl.ds(pl.multiple_of(pg * krows, krows), krows)],
                                     kbuf.at[step % 3, pl.ds(p * krows, krows)], sem_k.at[step % 3])

    def v_copy(step, p):
        pg = pt_ref[step // nj, (step % nj) * pages + p]
        return pltpu.make_async_copy(v_hbm.at[pl.ds(pl.multiple_of(pg * vrows, vrows), vrows)],
                                     vbuf.at[step % 3, pl.ds(p * vrows, vrows)], sem_v.at[step % 3])

    def fetch(step):
        for p in range(pages):
            k_copy(step, p).start()
            v_copy(step, p).start()

    @pl.when(t == 0)
    def _():
        fetch(0)

        @pl.when(n_steps > 1)
        def _():
            fetch(1)

    @pl.when(t + 2 < n_steps)
    def _():
        fetch(t + 2)

    slot = t % 3
    pltpu.make_async_copy(k_hbm.at[pl.ds(0, pages * krows)], kbuf.at[slot], sem_k.at[slot]).wait()
    pltpu.make_async_copy(v_hbm.at[pl.ds(0, pages * vrows)], vbuf.at[slot], sem_v.at[slot]).wait()

    @pl.when(j == 0)
    def _():
        m_s[...] = jnp.full(m_s.shape, -jnp.inf, F32)
        l_s[...] = jnp.zeros(l_s.shape, F32)
        acc_s[...] = jnp.zeros(acc_s.shape, F32)

    half_rows = DA_HEADS * SUBLANES
    q = q_ref[...]

    def k_page(p):
        return kbuf[slot, p * krows:(p + 1) * krows, :].astype(BF16)

    def v_page(p):
        return jnp.concatenate([vbuf[slot, pl.ds(p * vrows + h, page, stride=DA_HEADS), :] for h in range(DA_HEADS)],
                               axis=1).astype(BF16)

    s = jnp.concatenate([_dot(q, k_page(p)) for p in range(pages)], axis=1)
    m_prev = m_s[...]
    m_new = jnp.maximum(m_prev, jnp.max(s, axis=-1, keepdims=True))
    alpha = jnp.exp2(m_prev - m_new)
    pr = jnp.exp2(s - _lanes(m_new, s.shape[1]))
    pv = sum(_dot(pr[:, p * page:(p + 1) * page].astype(BF16), v_page(p)) for p in range(pages))
    l_s[...] = alpha * l_s[...] + jnp.sum(pr, axis=-1, keepdims=True)
    acc_s[...] = _lanes(alpha, acc_s.shape[1]) * acc_s[...] + pv
    m_s[...] = m_new

    @pl.when(j == nj - 1)
    def _():
        sn = _dot(q, kn_ref[...])
        tt = jnp.minimum(lax.broadcasted_iota(I32, sn.shape, 0) % SUBLANES, n_new - 1)
        cidx = lax.broadcasted_iota(I32, sn.shape, 1)
        _online_softmax_step(jnp.where(cidx <= tt, sn, -jnp.inf), vn_ref[...], m_s, l_s, acc_s)
        o2 = acc_s[...] / _lanes(l_s[...], acc_s.shape[1])
        lam = _lambda(lamv_ref)
        outs = []
        for h in range(DA_HEADS):
            r0 = h * SUBLANES
            o0 = o2[r0:r0 + SUBLANES, h * DA_V:(h + 1) * DA_V]
            o1 = o2[half_rows + r0:half_rows + r0 + SUBLANES, h * DA_V:(h + 1) * DA_V]
            outs.append(_subln(o0 - lam * o1, gsub_ref))
        o_ref[...] = jnp.concatenate(outs, axis=1)


def _attn_sample(page_table, qbd, cache_kt, cache_v2, knt, vn, lamv, gsub, *, n_new, page):
    bs, npg = page_table.shape
    pages = min(PAGES_PER_STEP, npg)
    while npg % pages:
        pages -= 1
    rows, width = qbd.shape[1], qbd.shape[2]
    vrows = page * DA_HEADS

    def seq3(b, j, pt):
        return (b, 0, 0)

    def const(b, j, pt):
        return (0, 0)

    in_specs = [pl.BlockSpec((None, rows, width), seq3),
                pl.BlockSpec(memory_space=pl.ANY), pl.BlockSpec(memory_space=pl.ANY),
                pl.BlockSpec((None, width, NEW_KV_PAD), seq3), pl.BlockSpec((None, NEW_KV_PAD, width), seq3),
                pl.BlockSpec(lamv.shape, const), pl.BlockSpec((1, DA_V), const)]
    grid_spec = pltpu.PrefetchScalarGridSpec(
        num_scalar_prefetch=1, grid=(bs, npg // pages), in_specs=in_specs,
        out_specs=pl.BlockSpec((None, SUBLANES, width), seq3),
        scratch_shapes=[pltpu.VMEM((3, pages * width, page), F32), pltpu.VMEM((3, pages * vrows, DA_V), F32),
                        pltpu.VMEM((rows, LANES), F32), pltpu.VMEM((rows, LANES), F32), pltpu.VMEM((rows, width), F32),
                        pltpu.SemaphoreType.DMA((3,)), pltpu.SemaphoreType.DMA((3,))])
    return pl.pallas_call(
        functools.partial(_dec_attn_kernel, pages=pages, page=page, n_new=n_new), grid_spec=grid_spec,
        out_shape=jax.ShapeDtypeStruct((bs, SUBLANES, width), F32),
        compiler_params=_params("arbitrary", "arbitrary"), name="attn_sample",
    )(page_table, qbd, cache_kt, cache_v2, knt, vn, lamv, gsub)


def _merge_kernel(xp_ref, xs_ref, hgp_ref, hgs_ref, op_ref, os_ref, sga_ref, sgb_ref, wa_ref, wb_ref, wo_ref,
                  gffn_ref, wr_ref, br_ref, x1_ref, lg_ref, *, n_prompt_tiles):
    is_prompt = pl.program_id(0) < n_prompt_tiles
    x = jnp.where(is_prompt, xp_ref[...], xs_ref[...])
    hg = jnp.where(is_prompt, hgp_ref[...], hgs_ref[...])
    o = jnp.where(is_prompt, op_ref[...], os_ref[...])
    mixed = sga_ref[...].astype(F32) * _dot(hg, wa_ref[...]) + sgb_ref[...].astype(F32) * _dot(o, wb_ref[...])
    x1 = x + _dot(mixed.astype(BF16), wo_ref[...])
    _slab_store(x1_ref, x1, x1.shape[1] // LANES)
    xn = (_rms(x1, NORM_EPS) * gffn_ref[...]).astype(BF16)
    lg_ref[...] = _dot_nt(wr_ref[...], xn) + br_ref[...]


def _merge(x_p, x_s, hg_p, hg_s, o_p, o_s, sga, sgb, wa, wb, wo, g_ffn, wr, br, tm):
    tp, d = x_p.shape
    ts = x_s.shape[0]
    npt, nst = tp // tm, ts // tm
    t_all = tp + ts

    def tok(i):
        return (i, 0)

    def const(i):
        return (0, 0)

    def p_idx(i):
        return (jnp.minimum(i, npt - 1), 0)

    def s_idx(i):
        return (jnp.maximum(i - npt, 0), 0)

    wv, wo_in = hg_p.shape[1], o_p.shape[1]
    in_specs = [pl.BlockSpec((tm, d), p_idx), pl.BlockSpec((tm, d), s_idx),
                pl.BlockSpec((tm, wv), p_idx), pl.BlockSpec((tm, wv), s_idx),
                pl.BlockSpec((tm, wo_in), p_idx), pl.BlockSpec((tm, wo_in), s_idx),
                pl.BlockSpec((tm, d), tok), pl.BlockSpec((tm, d), tok),
                pl.BlockSpec(wa.shape, const), pl.BlockSpec(wb.shape, const), pl.BlockSpec(wo.shape, const),
                pl.BlockSpec((1, d), const), pl.BlockSpec(wr.shape, const), pl.BlockSpec((ROUTER_ROWS, 1), const)]
    ch = d // LANES
    out_shape = (jax.ShapeDtypeStruct((t_all * ch, LANES), F32), jax.ShapeDtypeStruct((ROUTER_ROWS, t_all), F32))
    out_specs = (pl.BlockSpec((tm * ch, LANES), tok), pl.BlockSpec((ROUTER_ROWS, tm), lambda i: (0, i)))
    return pl.pallas_call(
        functools.partial(_merge_kernel, n_prompt_tiles=npt),
        grid=(npt + nst,), in_specs=in_specs, out_specs=out_specs, out_shape=out_shape,
        compiler_params=_params("arbitrary"), name="merge",
    )(x_p, x_s, hg_p, hg_s, o_p, o_s, sga, sgb, wa, wb, wo, g_ffn, wr, br)


def _route_kernel(lg_ref, eid_ref, gw_ref):
    x = lg_ref[...]
    sub = lax.broadcasted_iota(I32, (SUBLANES, x.shape[1]), 0)
    lg = jnp.where(sub < MOE_GROUPS, x[0:SUBLANES, :], -jnp.inf)
    gmax = jnp.max(lg, axis=0, keepdims=True)
    g_star = jnp.min(jnp.where(lg == gmax, sub, SUBLANES), axis=0, keepdims=True)
    pg_top = 1.0 / jnp.sum(jnp.exp(lg - gmax), axis=0, keepdims=True)
    le = x[SUBLANES:2 * SUBLANES, :]
    for g in range(1, MOE_GROUPS):
        le = jnp.where(g_star == g, x[(g + 1) * SUBLANES:(g + 2) * SUBLANES, :], le)
    ex = jnp.exp(le - jnp.max(le, axis=0, keepdims=True))
    pe = ex / jnp.sum(ex, axis=0, keepdims=True)
    v1 = jnp.max(pe, axis=0, keepdims=True)
    i1 = jnp.min(jnp.where(pe == v1, sub, SUBLANES), axis=0, keepdims=True)
    rest = jnp.where(sub == i1, -jnp.inf, pe)
    v2 = jnp.max(rest, axis=0, keepdims=True)
    i2 = jnp.min(jnp.where(rest == v2, sub, SUBLANES), axis=0, keepdims=True)
    tot = v1 + v2
    e1 = g_star * MOE_PER_GROUP + i1
    e2 = g_star * MOE_PER_GROUP + i2
    w1 = pg_top * (v1 / tot)
    w2 = pg_top * (v2 / tot)
    eid_ref[...] = jnp.where(sub == 0, e1, jnp.where(sub == 1, e2, 0))
    gw_ref[...] = jnp.where(sub == 0, w1, jnp.where(sub == 1, w2, 0.0))


def _route(lg):
    rows, t_all = lg.shape
    tb = ROUTE_TILE
    while t_all % tb:
        tb //= 2
    spec = pl.BlockSpec((SUBLANES, tb), lambda i: (0, i))
    return pl.pallas_call(
        _route_kernel, grid=(t_all // tb,),
        in_specs=[pl.BlockSpec((rows, tb), lambda i: (0, i))], out_specs=(spec, spec),
        out_shape=(jax.ShapeDtypeStruct((SUBLANES, t_all), I32), jax.ShapeDtypeStruct((SUBLANES, t_all), F32)),
        compiler_params=_params("arbitrary"), name="route",
    )(lg)


def _expert_kernel(iblk_ref, iexp_ref, ilo_ref, ihi_ref, nit_ref, idx_hbm, x_hbm, w_ref, gffn_ref, wg_ref, wu_ref, wd_ref,
                   y_hbm, idx_s, xbuf, ybuf, wg_s, wu_s, wd_s, sem_i, sem_g, sem_s, *, rows, n_blocks, chunks):
    i = pl.program_id(0)
    valid = i < nit_ref[0]
    blk = iblk_ref[i]
    lo = ilo_ref[i]
    hi = ihi_ref[i]
    first = lo == 0
    last = hi == rows
    xs = blk % 2

    def idx_copy(b):
        return pltpu.make_async_copy(idx_hbm.at[b], idx_s.at[pl.ds((b % 3) * 2 * rows, 2 * rows)], sem_i.at[b % 3])

    def issue_gather(b):
        slot = b % 2
        base = (b % 3) * 2 * rows

        def body(r, carry):
            src = pl.multiple_of(idx_s[base + r] * chunks, chunks)
            pltpu.make_async_copy(x_hbm.at[pl.ds(src, chunks)], xbuf.at[slot, pl.ds(pl.multiple_of(r * chunks, chunks), chunks)],
                                  sem_g.at[slot]).start()
            return carry

        lax.fori_loop(0, rows, body, 0, unroll=DMA_ISSUE_UNROLL)

    def wait_gather(slot):
        pltpu.make_async_copy(x_hbm.at[pl.ds(0, rows * chunks)], xbuf.at[slot], sem_g.at[slot]).wait()

    def issue_scatter(b):
        slot = b % 2
        base = (b % 3) * 2 * rows + rows

        def body(r, carry):
            dst = pl.multiple_of(idx_s[base + r] * chunks, chunks)
            pltpu.make_async_copy(ybuf.at[slot, pl.ds(pl.multiple_of(r * chunks, chunks), chunks)], y_hbm.at[pl.ds(dst, chunks)],
                                  sem_s.at[slot]).start()
            return carry

        lax.fori_loop(0, rows, body, 0, unroll=DMA_ISSUE_UNROLL)

    def wait_scatter(slot):
        pltpu.make_async_copy(ybuf.at[slot], y_hbm.at[pl.ds(0, rows * chunks)], sem_s.at[slot]).wait()

    @pl.when(jnp.logical_and(valid, i == 0))
    def _():
        idx_copy(0).start()
        idx_copy(0).wait()
        if n_blocks > 1:
            idx_copy(1).start()
        issue_gather(0)

    @pl.when(jnp.logical_and(valid, first))
    def _():
        wait_gather(xs)

        @pl.when(blk + 1 < n_blocks)
        def _():
            idx_copy(blk + 1).wait()
            issue_gather(blk + 1)

        @pl.when(blk + 2 < n_blocks)
        def _():
            idx_copy(blk + 2).start()

        @pl.when(blk >= 2)
        def _():
            wait_scatter(xs)

    changed = jnp.logical_or(i == 0, iexp_ref[i] != iexp_ref[jnp.maximum(i - 1, 0)])

    @pl.when(jnp.logical_and(valid, changed))
    def _():
        wg_s[...] = wg_ref[...].astype(BF16)
        wu_s[...] = wu_ref[...].astype(BF16)
        wd_s[...] = wd_ref[...].astype(BF16)

    def compute():
        xn = (_rms(_slab_load(xbuf, rows, chunks, lead=xs), NORM_EPS) * gffn_ref[...]).astype(BF16)
        g = _dot(xn, wg_s[...])
        u = _dot(xn, wu_s[...])
        hmid = (g * _sigmoid(g) * u).astype(BF16)
        r = lax.broadcasted_iota(I32, (rows, 1), 0)
        wrow = jnp.where(jnp.logical_and(r >= lo, r < hi), w_ref[...], 0.0)
        return _dot(hmid, wd_s[...]) * wrow

    @pl.when(jnp.logical_and(valid, first))
    def _():
        _slab_store(ybuf, compute(), chunks, lead=xs)

    @pl.when(jnp.logical_and(valid, jnp.logical_not(first)))
    def _():
        _slab_store(ybuf, _slab_load(ybuf, rows, chunks, lead=xs) + compute(), chunks, lead=xs)

    @pl.when(jnp.logical_and(valid, last))
    def _():
        issue_scatter(blk)

    @pl.when(jnp.logical_and(valid, i == nit_ref[0] - 1))
    def _():
        wait_scatter(xs)

        @pl.when(blk >= 1)
        def _():
            wait_scatter(1 - xs)


def _experts(item_blk, item_exp, item_lo, item_hi, n_items, idx_rows, x1_slab, w_rows, g_ffn, w_gate, w_up, w_down, rows):
    n_blocks = idx_rows.shape[0]
    d, ff = w_gate.shape[-2], w_gate.shape[-1]
    chunks = d // LANES
    n_out = MOE_TOP_K * x1_slab.shape[0]
    n_max = item_blk.shape[0]

    def blk_idx(i, ib, ie, il, ih, nt):
        return (ib[i], 0)

    def const(i, ib, ie, il, ih, nt):
        return (0, 0)

    def w_idx(i, ib, ie, il, ih, nt):
        return (ie[i], 0, 0)

    grid_spec = pltpu.PrefetchScalarGridSpec(
        num_scalar_prefetch=5, grid=(n_max,),
        in_specs=[pl.BlockSpec(memory_space=pl.ANY), pl.BlockSpec(memory_space=pl.ANY),
                  pl.BlockSpec((rows, 1), blk_idx), pl.BlockSpec((1, d), const),
                  pl.BlockSpec((None, d, ff), w_idx), pl.BlockSpec((None, d, ff), w_idx),
                  pl.BlockSpec((None, ff, d), w_idx)],
        out_specs=pl.BlockSpec(memory_space=pl.ANY),
        scratch_shapes=[pltpu.SMEM((3 * 2 * rows,), I32),
                        pltpu.VMEM((2, rows * chunks, LANES), F32), pltpu.VMEM((2, rows * chunks, LANES), F32),
                        pltpu.VMEM((d, ff), BF16), pltpu.VMEM((d, ff), BF16), pltpu.VMEM((ff, d), BF16),
                        pltpu.SemaphoreType.DMA((3,)), pltpu.SemaphoreType.DMA((2,)), pltpu.SemaphoreType.DMA((2,))])
    return pl.pallas_call(
        functools.partial(_expert_kernel, rows=rows, n_blocks=n_blocks, chunks=chunks), grid_spec=grid_spec,
        out_shape=jax.ShapeDtypeStruct((n_out, LANES), F32),
        compiler_params=_params("arbitrary"), name="experts",
    )(item_blk, item_exp, item_lo, item_hi, n_items, idx_rows, x1_slab, w_rows, g_ffn, w_gate, w_up, w_down)


def _final_kernel(x1_ref, y0_ref, y1_ref, plep_ref, ples_ref, gple_ref, wpg_ref, wpp_ref, gfin_ref, yp_ref, ys_ref,
                  *, n_prompt_tiles):
    is_prompt = pl.program_id(0) < n_prompt_tiles
    tm, ch = plep_ref.shape[0], gple_ref.shape[1] // LANES
    x2 = _slab_load(x1_ref, tm, ch) + (_slab_load(y0_ref, tm, ch) + _slab_load(y1_ref, tm, ch))
    xn = (_rms(x2, NORM_EPS) * gple_ref[...]).astype(BF16)
    ple = jnp.where(is_prompt, plep_ref[...], ples_ref[...]).astype(BF16)
    x3 = x2 + _sigmoid(_dot(xn, wpg_ref[...])) * _dot(ple, wpp_ref[...])
    y = _rms(x3, NORM_EPS) * gfin_ref[...]

    @pl.when(is_prompt)
    def _():
        yp_ref[...] = y

    @pl.when(jnp.logical_not(is_prompt))
    def _():
        ys_ref[...] = y


def _final(x1, y_slots, ple_p, ple_s, g_ple, wpg, wpp, g_final, tm):
    d = g_ple.shape[1]
    ch = d // LANES
    tp, ts = ple_p.shape[0], ple_s.shape[0]
    npt, nst = tp // tm, ts // tm
    pd = ple_p.shape[1]
    nt = npt + nst

    def tok(i):
        return (i, 0)

    def tok1(i):
        return (nt + i, 0)

    def const(i):
        return (0, 0)

    def p_idx(i):
        return (jnp.minimum(i, npt - 1), 0)

    def s_idx(i):
        return (jnp.maximum(i - npt, 0), 0)

    in_specs = [pl.BlockSpec((tm * ch, LANES), tok), pl.BlockSpec((tm * ch, LANES), tok),
                pl.BlockSpec((tm * ch, LANES), tok1),
                pl.BlockSpec((tm, pd), p_idx), pl.BlockSpec((tm, pd), s_idx),
                pl.BlockSpec((1, d), const), pl.BlockSpec(wpg.shape, const), pl.BlockSpec(wpp.shape, const),
                pl.BlockSpec((1, d), const)]
    return pl.pallas_call(
        functools.partial(_final_kernel, n_prompt_tiles=npt),
        grid=(nt,), in_specs=in_specs,
        out_specs=(pl.BlockSpec((tm, d), p_idx), pl.BlockSpec((tm, d), s_idx)),
        out_shape=(jax.ShapeDtypeStruct((tp, d), F32), jax.ShapeDtypeStruct((ts, d), F32)),
        compiler_params=_params("arbitrary"), name="final",
    )(x1, y_slots, y_slots, ple_p, ple_s, g_ple, wpg, wpp, g_final)


def _rope_tables(pos):
    half = DA_QK // 2
    inv = ROPE_THETA ** (-jnp.arange(half, dtype=F32) / half)
    ang = pos.astype(F32)[:, None] * inv[None, :]
    cos, sin = jnp.cos(ang), jnp.sin(ang)
    reps = LANES // DA_QK
    cos_t = jnp.tile(jnp.concatenate([cos, cos], axis=1), (1, reps))
    sin_t = jnp.tile(jnp.concatenate([-sin, sin], axis=1), (1, reps))
    return cos_t, sin_t


def _tile(limit, *sizes):
    t = limit
    while any(s % t for s in sizes):
        t //= 2
    return t


def kernel(x_prompt, x_sample, cache_k, cache_v, state_mlstm_C, state_mlstm_n, state_mlstm_m, page_table, p_prompt, p_sample, g_mix, w_in, b_ml_i, b_ml_f, lam_q1, lam_k1, lam_q2, lam_k2, g_sub, w_br_a, w_br_b, w_out, g_ffn, w_rg, b_rg, w_re, b_re, w_e_gate, w_e_up, w_e_down, g_ple, w_ple_gate, w_ple_proj, g_final):
    depth = w_in.shape[0]
    assert depth == 1, "single-layer step"
    bp, sp, d = x_prompt.shape
    bs, ss, _ = x_sample.shape
    assert ss <= SUBLANES
    tp, ts = bp * sp, bs * ss
    t_all = tp + ts
    n_pages, page = page_table.shape[1], cache_k.shape[2]
    past_len = n_pages * page
    w = ML_HEADS * ML_DK
    aw = DA_HEADS * 2 * DA_QK
    li = 0

    wi = w_in[li]
    sizes = (w, w, ML_HEADS * ML_DV, ML_HEADS * ML_DV, ML_HEADS, ML_HEADS, aw, aw, DA_HEADS * DA_V, d, d)
    edges = [0]
    for n in sizes:
        edges.append(edges[-1] + n)
    assert edges[-1] == wi.shape[1]
    seg = [wi[:, edges[i]:edges[i + 1]] for i in range(11)]
    w_main = jnp.concatenate(seg[0:4] + [seg[6]] + seg[8:11], axis=1).astype(BF16)
    w_kt = seg[7].T.astype(BF16)
    w_gates = jnp.concatenate([seg[4], seg[5]], axis=1)
    w_gr = jnp.pad(w_gates.T, ((0, BF16_SUBLANES - 2 * ML_HEADS), (0, 0))).astype(BF16)
    w_gc = jnp.pad(w_gates, ((0, 0), (0, LANES - 2 * ML_HEADS))).astype(BF16)
    b_gates = jnp.concatenate([b_ml_i[li], b_ml_f[li]]).astype(F32)
    b_row = jnp.pad(b_gates, (0, BF16_SUBLANES - 2 * ML_HEADS))[:, None]
    b_col = jnp.pad(b_gates, (0, LANES - 2 * ML_HEADS))[None, :]
    lamv = jnp.stack([lam_q1[li], lam_k1[li], lam_q2[li], lam_k2[li]]).astype(F32)
    gsub = g_sub[li][None, :].astype(F32)
    w_router = jnp.zeros((ROUTER_ROWS, d), F32).at[0:MOE_GROUPS].set(w_rg[li].T).at[SUBLANES:SUBLANES + MOE_EXPERTS].set(w_re[li].T)
    b_router = jnp.zeros((ROUTER_ROWS,), F32).at[0:MOE_GROUPS].set(b_rg[li]).at[SUBLANES:SUBLANES + MOE_EXPERTS].set(b_re[li])

    tm = _tile(TOKEN_TILE, sp, ts)
    cos_p, sin_p = _rope_tables(jnp.arange(sp))
    cos_s, sin_s = _rope_tables(past_len + (jnp.arange(tm) % ss))
    cos_t = jnp.concatenate([cos_p, cos_s], axis=0)
    sin_t = jnp.concatenate([sin_p, sin_s], axis=0)
    cos_tt = cos_t[:, :DA_QK].T
    sin_tt = sin_t[:, :DA_QK].T
    xp2 = x_prompt.reshape(tp, d)
    xs2 = x_sample.reshape(ts, d)
    (q_ml, k_ml, v_ml, og, grow, gcol, qa, vab, sga, sgb, kt_p, ktb_p, v_p, kt_s, ktb_s, v_s) = _inproj(
        xp2, xs2, g_mix[li][None, :], w_main, w_kt, w_gr, w_gc, b_row, b_col, cos_t, sin_t, cos_tt, sin_tt, tm, sp)

    chunk = _tile(MLSTM_CHUNK, sp)
    ncp = sp // chunk
    grow3_p = grow[:, :tp].reshape(BF16_SUBLANES, bp * ncp, chunk).transpose(1, 0, 2)
    zc = jnp.zeros((bp, ML_HEADS, ML_DV, ML_DK), F32)
    zn = jnp.zeros((bp, ML_HEADS, ML_DK), F32)
    zm = jnp.zeros((bp, ML_HEADS, LANES), F32)
    hg_p, c_p, n_p, m_p = _mlstm(q_ml, k_ml, v_ml, og, grow3_p, gcol, zc, zn, zm,
                                 batch=bp, chunk=chunk, row_block_offset=0)

    padn = SAMPLE_PAD - ss

    def pad_seq(a):
        return jnp.pad(a[tp:].reshape(bs, ss, -1), ((0, 0), (0, padn), (0, 0))).reshape(bs * SAMPLE_PAD, -1)

    neutral = jnp.where(jnp.arange(LANES) < ML_HEADS, -1e30, 0.0).astype(F32)
    gcol_s = jnp.concatenate([gcol[tp:].reshape(bs, ss, LANES),
                              jnp.broadcast_to(neutral, (bs, padn, LANES))], axis=1).reshape(bs * SAMPLE_PAD, LANES)
    grow3_s = jnp.concatenate([grow[:, tp:].reshape(BF16_SUBLANES, bs, ss).transpose(1, 0, 2),
                               jnp.broadcast_to(neutral[:BF16_SUBLANES, None], (bs, BF16_SUBLANES, padn))], axis=2)
    m0_s = jnp.broadcast_to(state_mlstm_m[li].astype(F32)[:, :, None], (bs, ML_HEADS, LANES))
    hg_s_pad, c_s, n_s, m_s = _mlstm(pad_seq(q_ml), pad_seq(k_ml), pad_seq(v_ml), pad_seq(og), grow3_s, gcol_s,
                                     state_mlstm_C[li].astype(F32), state_mlstm_n[li].astype(F32), m0_s,
                                     batch=bs, chunk=SAMPLE_PAD, row_block_offset=0)
    hg_s = hg_s_pad.reshape(bs, SAMPLE_PAD, -1)[:, :ss].reshape(ts, -1)

    blk = _tile(ATTN_BLOCK, sp)
    o_p = _attn_prompt(qa, ktb_p, vab, lamv, gsub, batch=bp, seq=sp, blk=blk)

    q_s = jnp.pad(qa[tp:].reshape(bs, ss, DA_HEADS, 2, DA_QK), ((0, 0), (0, SUBLANES - ss), (0, 0), (0, 0), (0, 0)))
    q_cht = q_s.transpose(0, 3, 2, 1, 4)
    same = jnp.logical_and(
        (jnp.arange(DA_HEADS)[:, None] == jnp.arange(DA_HEADS)[None, :])[None, None, :, None, :, None, None],
        (jnp.arange(2)[:, None] == jnp.arange(2)[None, :])[None, :, None, None, None, :, None])
    qbd = jnp.where(same, q_cht[:, :, :, :, None, None, :], jnp.zeros((), BF16)).reshape(bs, 2 * DA_HEADS * SUBLANES, aw)
    knt = jnp.pad(ktb_s.reshape(aw, bs, ss).transpose(1, 0, 2), ((0, 0), (0, 0), (0, NEW_KV_PAD - ss)))
    vn = jnp.pad(vab[tp:].reshape(bs, ss, aw), ((0, 0), (0, NEW_KV_PAD - ss), (0, 0)))
    ckt = cache_k[li].transpose(0, 2, 3, 4, 1).reshape(-1, page)
    cv2 = cache_v[li].reshape(-1, DA_V)
    o_s = _attn_sample(page_table, qbd, ckt, cv2, knt, vn, lamv, gsub, n_new=ss, page=page)
    o_s = o_s[:, :ss].reshape(ts, aw).astype(BF16)

    x1, lg = _merge(xp2, xs2, hg_p, hg_s, o_p, o_s, sga, sgb,
                    w_br_a[li].astype(BF16), w_br_b[li].astype(BF16), w_out[li].astype(BF16),
                    g_ffn[li][None, :], w_router.astype(BF16), b_router[:, None], tm)

    eid8, gw8 = _route(lg)
    n_slots = t_all * MOE_TOP_K
    rows = _tile(EXPERT_ROWS, n_slots)
    nb = n_slots // rows
    flat_e = eid8[:MOE_TOP_K].T.reshape(-1)
    flat_w = gw8[:MOE_TOP_K].T.reshape(-1)
    slot_bits = max(1, (n_slots - 1).bit_length())
    assert MOE_EXPERTS << slot_bits < 2 ** 31
    key = (flat_e << slot_bits) | jnp.arange(n_slots, dtype=I32)
    sorted_key, sorted_w = lax.sort((key, flat_w), num_keys=1)
    order = sorted_key & ((1 << slot_bits) - 1)
    tok_rows = (order // MOE_TOP_K).reshape(nb, rows)
    dst_rows = ((order % MOE_TOP_K) * t_all + order // MOE_TOP_K).reshape(nb, rows)
    idx_rows = jnp.concatenate([tok_rows, dst_rows], axis=1)
    counts = jnp.sum(flat_e[None, :] == jnp.arange(MOE_EXPERTS, dtype=I32)[:, None], axis=1, dtype=I32)
    ends = jnp.cumsum(counts)
    starts = ends - counts
    first_blk = starts // rows
    n_e = jnp.where(counts > 0, (ends - 1) // rows - first_blk + 1, 0)
    item_end = jnp.cumsum(n_e)
    item_start = item_end - n_e
    n_items = item_end[-1:]
    n_max = nb + MOE_EXPERTS - 1
    it = jnp.minimum(jnp.arange(n_max, dtype=I32), n_items[0] - 1)
    item_exp = jnp.sum(item_end[None, :] <= it[:, None], axis=1, dtype=I32)
    item_blk = first_blk[item_exp] + it - item_start[item_exp]
    item_lo = jnp.maximum(starts[item_exp] - item_blk * rows, 0)
    item_hi = jnp.minimum(ends[item_exp] - item_blk * rows, rows)
    y_slots = _experts(item_blk, item_exp, item_lo, item_hi, n_items, idx_rows, x1, sorted_w[:, None],
                       g_ffn[li][None, :], w_e_gate[li], w_e_up[li], w_e_down[li], rows)

    pd = p_prompt.shape[-1]
    y_p, y_s = _final(x1, y_slots, p_prompt[li].reshape(tp, pd), p_sample[li].reshape(ts, pd), g_ple[li][None, :],
                      w_ple_gate[li].astype(BF16), w_ple_proj[li].astype(BF16), g_final[None, :], tm)

    return (y_p.reshape(bp, sp, d), y_s.reshape(bs, ss, d),
            kt_p.reshape(1, bp, DA_HEADS, 2, DA_QK, sp).transpose(0, 1, 5, 2, 3, 4), v_p.reshape(1, bp, sp, DA_HEADS, DA_V),
            c_p[None], n_p[None], m_p[None, :, :, 0],
            kt_s.reshape(1, DA_HEADS, 2, DA_QK, bs, ss).transpose(0, 4, 5, 1, 2, 3), v_s.reshape(1, bs, ss, DA_HEADS, DA_V),
            c_s[None], n_s[None], m_s[None, :, :, 0])
```

```python
import functools
import math

import jax
import jax.numpy as jnp
from jax import lax
from jax.experimental import pallas as pl
from jax.experimental.pallas import tpu as pltpu

F32 = jnp.float32
BF16 = jnp.bfloat16
I32 = jnp.int32

ML_HEADS = 4
ML_DK = 128
ML_DV = 128
DA_HEADS = 4
DA_QK = 64
DA_V = 128
ROPE_THETA = 10000.0
MOE_GROUPS = 4
MOE_PER_GROUP = 8
MOE_EXPERTS = MOE_GROUPS * MOE_PER_GROUP
MOE_TOP_K = 2
NORM_EPS = 1e-6
SUBLN_EPS = 1e-5
LAYER_INDEX = 0
LAM_INIT = 0.8 - 0.6 * math.exp(-0.3 * LAYER_INDEX)
LOG2E = math.log2(math.e)

LANES = 128
SUBLANES = 8
BF16_SUBLANES = 16
VMEM_LIMIT_BYTES = 56 * 1024 * 1024

TOKEN_TILE = 512
MLSTM_CHUNK = 256
ATTN_BLOCK = 512
PAGES_PER_STEP = 16
EXPERT_ROWS = 512
ROUTE_TILE = 512
ROUTER_ROWS = 48
SAMPLE_PAD = 16
NEW_KV_PAD = 128
DMA_ISSUE_UNROLL = 16

NT_DIMS = (((1,), (1,)), ((), ()))
TN_DIMS = (((0,), (0,)), ((), ()))


def _params(*sem):
    return pltpu.CompilerParams(dimension_semantics=sem, vmem_limit_bytes=VMEM_LIMIT_BYTES)


def _sigmoid(x):
    return 1.0 / (1.0 + jnp.exp(-x))


def _log_sigmoid(x):
    return jnp.minimum(x, 0.0) - jnp.log1p(jnp.exp(-jnp.abs(x)))


def _rms(x, eps):
    return x * lax.rsqrt(jnp.mean(x * x, axis=-1, keepdims=True) + eps)


def _dot(a, b):
    return jnp.dot(a, b, preferred_element_type=F32)


def _dot_nt(a, b):
    return lax.dot_general(a, b, NT_DIMS, preferred_element_type=F32)


def _dot_tn(a, b):
    return lax.dot_general(a, b, TN_DIMS, preferred_element_type=F32)


def _slab_load(ref, rows, chunks, lead=None):
    def piece(c):
        idx = (pl.ds(c, rows, stride=chunks), slice(None))
        return ref[idx] if lead is None else ref[(lead,) + idx]
    return jnp.concatenate([piece(c) for c in range(chunks)], axis=1)


def _slab_store(ref, val, chunks, lead=None):
    rows = val.shape[0]
    for c in range(chunks):
        idx = (pl.ds(c, rows, stride=chunks), slice(None))
        ref[idx if lead is None else (lead,) + idx] = val[:, c * LANES:(c + 1) * LANES]


def _split3(a):
    hi = a.astype(BF16)
    r1 = a - hi.astype(F32)
    mid = r1.astype(BF16)
    lo = (r1 - mid.astype(F32)).astype(BF16)
    return hi, mid, lo


def _inproj_kernel(xp_ref, xs_ref, g_ref, wm_ref, wkt_ref, wgr_ref, wgc_ref, br_ref, bc_ref, cos_ref, sin_ref,
                   cost_ref, sint_ref,
                   q_ref, k_ref, v_ref, og_ref, grow_ref, gcol_ref, qa_ref, vab_ref, sga_ref, sgb_ref,
                   ktp_ref, ktbp_ref, vp_ref, kts_ref, ktbs_ref, vs_ref, *, n_prompt_tiles):
    i = pl.program_id(0)
    is_prompt = i < n_prompt_tiles
    x = jnp.where(is_prompt, xp_ref[...], xs_ref[...])
    xn = (_rms(x, NORM_EPS) * g_ref[...]).astype(BF16)

    def mm(lo, hi):
        return _dot(xn, wm_ref[:, lo:hi])

    w = ML_HEADS * ML_DK
    q_ref[...] = mm(0, w).astype(BF16)
    k_ref[...] = (mm(w, 2 * w) * (ML_DK ** -0.5)).astype(BF16)
    v_ref[...] = mm(2 * w, 3 * w).astype(BF16)
    og_ref[...] = _sigmoid(mm(3 * w, 4 * w)).astype(BF16)

    gr = _dot_nt(wgr_ref[...], xn) + br_ref[...]
    rr = lax.broadcasted_iota(I32, gr.shape, 0)
    grow_ref[...] = jnp.where(rr >= ML_HEADS, _log_sigmoid(gr), gr)
    gc = _dot(xn, wgc_ref[...]) + bc_ref[...]
    cc = lax.broadcasted_iota(I32, gc.shape, 1)
    gcol_ref[...] = jnp.where(cc >= ML_HEADS, _log_sigmoid(gc), gc)

    aw = DA_HEADS * 2 * DA_QK
    reps = aw // LANES
    cosv = jnp.concatenate([cos_ref[...]] * reps, axis=1)
    sinv = jnp.concatenate([sin_ref[...]] * reps, axis=1)
    half = DA_QK // 2
    base = 4 * w
    zq = mm(base, base + aw)
    lane = lax.broadcasted_iota(I32, zq.shape, 1)
    partner = jnp.where((lane % DA_QK) < half, pltpu.roll(zq, aw - half, axis=1), pltpu.roll(zq, half, axis=1))
    qa_ref[...] = ((zq * cosv + partner * sinv) * (DA_QK ** -0.5 * LOG2E)).astype(BF16)

    kgroups = aw // DA_QK
    zk = _dot_nt(wkt_ref[...], xn)
    pieces = []
    for g in range(kgroups):
        pieces += [zk[g * DA_QK + half:(g + 1) * DA_QK, :], zk[g * DA_QK:g * DA_QK + half, :]]
    zk_partner = jnp.concatenate(pieces, axis=0)
    kt = (zk * jnp.concatenate([cost_ref[...]] * kgroups, axis=0)
          + zk_partner * jnp.concatenate([sint_ref[...]] * kgroups, axis=0))

    va = mm(base + aw, base + 2 * aw)
    vab_ref[...] = va.astype(BF16)
    tm = va.shape[0]

    def store_kv(kt_out, ktb_out, v_out):
        kt_out[...] = kt
        ktb_out[...] = kt.astype(BF16)
        for h in range(DA_HEADS):
            v_out[pl.ds(h, tm, stride=DA_HEADS), :] = va[:, h * DA_V:(h + 1) * DA_V]

    @pl.when(is_prompt)
    def _():
        store_kv(ktp_ref, ktbp_ref, vp_ref)

    @pl.when(jnp.logical_not(is_prompt))
    def _():
        store_kv(kts_ref, ktbs_ref, vs_ref)

    base = base + 2 * aw
    d = g_ref.shape[-1]
    sga_ref[...] = _sigmoid(mm(base, base + d)).astype(BF16)
    sgb_ref[...] = _sigmoid(mm(base + d, base + 2 * d)).astype(BF16)


def _inproj(x_p, x_s, g_mix, w_main, w_kt, w_gr, w_gc, b_row, b_col, cos_t, sin_t, cos_tt, sin_tt, tm, seq):
    tp, d = x_p.shape
    ts = x_s.shape[0]
    npt, nst = tp // tm, ts // tm
    n_pos_tiles = seq // tm
    batch = tp // seq
    t_all = tp + ts
    w = ML_HEADS * ML_DK
    aw = DA_HEADS * 2 * DA_QK
    ncols = w_main.shape[1]

    def tok(i):
        return (i, 0)

    def const(i):
        return (0, 0)

    def p_idx(i):
        return (jnp.minimum(i, npt - 1), 0)

    def s_idx(i):
        return (jnp.maximum(i - npt, 0), 0)

    def pos_idx(i):
        return (jnp.where(i < npt, i % n_pos_tiles, n_pos_tiles), 0)

    def pos_idx_t(i):
        return (0, jnp.where(i < npt, i % n_pos_tiles, n_pos_tiles))

    def ktp_idx(i):
        ip = jnp.minimum(i, npt - 1)
        return (ip // n_pos_tiles, ip % n_pos_tiles)

    def kts_idx(i):
        return (0, jnp.maximum(i - npt, 0))

    bf = lambda n: jax.ShapeDtypeStruct((t_all, n), BF16)
    out_shape = (bf(w), bf(w), bf(w), bf(w),
                 jax.ShapeDtypeStruct((BF16_SUBLANES, t_all), F32),
                 jax.ShapeDtypeStruct((t_all, LANES), F32),
                 bf(aw), bf(aw), bf(d), bf(d),
                 jax.ShapeDtypeStruct((batch * aw, seq), F32), jax.ShapeDtypeStruct((batch * aw, seq), BF16),
                 jax.ShapeDtypeStruct((tp * DA_HEADS, DA_V), F32),
                 jax.ShapeDtypeStruct((aw, ts), F32), jax.ShapeDtypeStruct((aw, ts), BF16),
                 jax.ShapeDtypeStruct((ts * DA_HEADS, DA_V), F32))
    out_specs = (pl.BlockSpec((tm, w), tok), pl.BlockSpec((tm, w), tok), pl.BlockSpec((tm, w), tok),
                 pl.BlockSpec((tm, w), tok),
                 pl.BlockSpec((BF16_SUBLANES, tm), lambda i: (0, i)),
                 pl.BlockSpec((tm, LANES), tok),
                 pl.BlockSpec((tm, aw), tok), pl.BlockSpec((tm, aw), tok),
                 pl.BlockSpec((tm, d), tok), pl.BlockSpec((tm, d), tok),
                 pl.BlockSpec((aw, tm), ktp_idx), pl.BlockSpec((aw, tm), ktp_idx),
                 pl.BlockSpec((tm * DA_HEADS, DA_V), p_idx),
                 pl.BlockSpec((aw, tm), kts_idx), pl.BlockSpec((aw, tm), kts_idx),
                 pl.BlockSpec((tm * DA_HEADS, DA_V), s_idx))
    in_specs = [pl.BlockSpec((tm, d), p_idx), pl.BlockSpec((tm, d), s_idx),
                pl.BlockSpec((1, d), const),
                pl.BlockSpec((d, ncols), const, pipeline_mode=pl.Buffered(1)),
                pl.BlockSpec((aw, d), const, pipeline_mode=pl.Buffered(1)),
                pl.BlockSpec((BF16_SUBLANES, d), const),
                pl.BlockSpec((d, LANES), const),
                pl.BlockSpec((BF16_SUBLANES, 1), const),
                pl.BlockSpec((1, LANES), const),
                pl.BlockSpec((tm, LANES), pos_idx), pl.BlockSpec((tm, LANES), pos_idx),
                pl.BlockSpec((DA_QK, tm), pos_idx_t), pl.BlockSpec((DA_QK, tm), pos_idx_t)]
    return pl.pallas_call(
        functools.partial(_inproj_kernel, n_prompt_tiles=npt),
        grid=(npt + nst,), in_specs=in_specs, out_specs=out_specs, out_shape=out_shape,
        compiler_params=_params("arbitrary"), name="inproj",
    )(x_p, x_s, g_mix, w_main, w_kt, w_gr, w_gc, b_row, b_col, cos_t, sin_t, cos_tt, sin_tt)


def _mlstm_kernel(q_ref, k_ref, v_ref, og_ref, grow_ref, gcol_ref, c0_ref, n0_ref, m0_ref,
                  hg_ref, c_out, n_out, m_out, c_s, n_s, m_s, *, chunk):
    c = pl.program_id(1)
    nc = pl.num_programs(1)

    @pl.when(c == 0)
    def _():
        c_s[...] = c0_ref[...]
        n_s[...] = n0_ref[...]
        m_s[...] = m0_ref[...]

    L = chunk
    row = lax.broadcasted_iota(I32, (L, L), 0)
    col = lax.broadcasted_iota(I32, (L, L), 1)
    causal = col <= row
    tri = causal.astype(BF16)
    tri_t = (row <= col).astype(BF16)

    g_row = grow_ref[...]
    g_col = gcol_ref[...]
    cum_row = sum(_dot(p, tri_t) for p in _split3(g_row))
    cum_col = sum(_dot(tri, p) for p in _split3(g_col))

    for h in range(ML_HEADS):
        lo, hi = h * ML_DK, (h + 1) * ML_DK
        f = ML_HEADS + h
        b_rep = jnp.broadcast_to(cum_col[:, f:f + 1], (L, LANES))
        ig_rep = jnp.broadcast_to(g_col[:, h:h + 1], (L, LANES))
        b_row = cum_row[f:f + 1, :]
        ig_row = g_row[h:h + 1, :]
        m0 = m_s[h:h + 1, :]
        c0 = c_s[h]
        n0 = n_s[h:h + 1, :]
        qh = q_ref[:, lo:hi]
        kh = k_ref[:, lo:hi]
        vh = v_ref[:, h * ML_DV:(h + 1) * ML_DV]

        dmat = jnp.where(causal, _lanes(b_rep, L) - b_row + ig_row, -jnp.inf)
        inter = b_rep + m0
        mt = jnp.maximum(inter, jnp.max(dmat, axis=-1, keepdims=True))
        wts = jnp.exp(dmat - _lanes(mt, L)) * _dot_nt(qh, kh)
        decay0 = jnp.exp(inter - mt)
        num = _dot(wts.astype(BF16), vh) + decay0 * _dot_nt(qh, c0.astype(BF16))
        qn = jnp.sum(qh.astype(F32) * n0, axis=-1, keepdims=True)
        den = jnp.sum(wts, axis=-1, keepdims=True) + decay0 * qn
        hh = num / jnp.maximum(jnp.abs(den), jnp.exp(-mt))
        hg_ref[:, h * ML_DV:(h + 1) * ML_DV] = (hh * og_ref[:, h * ML_DV:(h + 1) * ML_DV].astype(F32)).astype(BF16)

        b_last = b_rep[L - 1:L, :]
        m_new = mt[L - 1:L, :]
        g_last = jnp.exp(b_last + m0 - m_new)
        ws = jnp.exp(b_last - b_rep + ig_rep - m_new)
        vw = (vh.astype(F32) * ws).astype(BF16)
        c_s[h] = g_last * c0 + _dot_tn(vw, kh)
        n_s[h:h + 1, :] = g_last * n0 + jnp.sum(kh.astype(F32) * ws, axis=0, keepdims=True)
        m_s[h:h + 1, :] = m_new

    @pl.when(c == nc - 1)
    def _():
        c_out[...] = c_s[...]
        n_out[...] = n_s[...]
        m_out[...] = m_s[...]


def _mlstm(q, k, v, og, grow3, gcol, c0, n0, m0, *, batch, chunk, row_block_offset):
    nchunks_total = grow3.shape[0]
    nc = nchunks_total // batch
    w = ML_HEADS * ML_DK
    wv = ML_HEADS * ML_DV

    def tok(b, c):
        return (row_block_offset + b * nc + c, 0)

    def tok0(b, c):
        return (b * nc + c, 0)

    def st4(b, c):
        return (b, 0, 0, 0)

    def st3(b, c):
        return (b, 0, 0)

    rows = batch * nc * chunk
    out_shape = (jax.ShapeDtypeStruct((rows, wv), BF16),
                 jax.ShapeDtypeStruct(c0.shape, F32),
                 jax.ShapeDtypeStruct(n0.shape, F32),
                 jax.ShapeDtypeStruct(m0.shape, F32))
    in_specs = [pl.BlockSpec((chunk, w), tok), pl.BlockSpec((chunk, w), tok), pl.BlockSpec((chunk, wv), tok),
                pl.BlockSpec((chunk, wv), tok),
                pl.BlockSpec((None, BF16_SUBLANES, chunk), lambda b, c: (b * nc + c, 0, 0)),
                pl.BlockSpec((chunk, LANES), tok),
                pl.BlockSpec((None, ML_HEADS, ML_DV, ML_DK), st4),
                pl.BlockSpec((None, ML_HEADS, ML_DK), st3),
                pl.BlockSpec((None, ML_HEADS, LANES), st3)]
    out_specs = (pl.BlockSpec((chunk, wv), tok0),
                 pl.BlockSpec((None, ML_HEADS, ML_DV, ML_DK), st4),
                 pl.BlockSpec((None, ML_HEADS, ML_DK), st3),
                 pl.BlockSpec((None, ML_HEADS, LANES), st3))
    scratch = [pltpu.VMEM((ML_HEADS, ML_DV, ML_DK), F32), pltpu.VMEM((ML_HEADS, ML_DK), F32),
               pltpu.VMEM((ML_HEADS, LANES), F32)]
    return pl.pallas_call(
        functools.partial(_mlstm_kernel, chunk=chunk),
        grid=(batch, nc), in_specs=in_specs, out_specs=out_specs, out_shape=out_shape,
        scratch_shapes=scratch, compiler_params=_params("arbitrary", "arbitrary"), name="mlstm",
    )(q, k, v, og, grow3, gcol, c0, n0, m0)


def _lambda(lamv_ref):
    lv = lamv_ref[...]
    s1 = jnp.sum(lv[0:1, :] * lv[1:2, :], axis=-1, keepdims=True)
    s2 = jnp.sum(lv[2:3, :] * lv[3:4, :], axis=-1, keepdims=True)
    return jnp.exp(s1) - jnp.exp(s2) + LAM_INIT


def _subln(o, gsub_ref):
    return _rms(o, SUBLN_EPS) * gsub_ref[...] * (1.0 - LAM_INIT)


def _lanes(a, width):
    if width <= LANES:
        return a[:, :width]
    return jnp.concatenate([a] * (width // LANES), axis=1)


def _online_softmax_step(s, v, m_s, l_s, acc_s):
    m_prev = m_s[...]
    m_new = jnp.maximum(m_prev, jnp.max(s, axis=-1, keepdims=True))
    alpha = jnp.exp2(m_prev - m_new)
    p = jnp.exp2(s - _lanes(m_new, s.shape[1]))
    l_s[...] = alpha * l_s[...] + jnp.sum(p, axis=-1, keepdims=True)
    acc_s[...] = _lanes(alpha, acc_s.shape[1]) * acc_s[...] + _dot(p.astype(BF16), v)
    m_s[...] = m_new


def _attn_kernel(qt_ref, kt_ref, q_ref, k_ref, v_ref, lamv_ref, gsub_ref, o_ref, q2_s, m_s, l_s, acc_s, *, blk):
    p = pl.program_id(1)
    qi = qt_ref[p]
    ki = kt_ref[p]
    hw = 2 * DA_QK

    @pl.when(ki == 0)
    def _():
        q = q_ref[...]
        lane = lax.broadcasted_iota(I32, q.shape, 1) % hw
        zero = jnp.zeros_like(q)
        qlo = jnp.where(lane < DA_QK, q, zero)
        qhi = jnp.where(lane >= DA_QK, q, zero)
        for h in range(DA_HEADS):
            q2_s[h, 0:blk, :] = qlo[:, h * hw:(h + 1) * hw]
            q2_s[h, blk:2 * blk, :] = qhi[:, h * hw:(h + 1) * hw]
        m_s[...] = jnp.full(m_s.shape, -jnp.inf, F32)
        l_s[...] = jnp.zeros(l_s.shape, F32)
        acc_s[...] = jnp.zeros(acc_s.shape, F32)

    def scores(h):
        return _dot(q2_s[h], k_ref[h * hw:(h + 1) * hw, :])

    @pl.when(ki < qi)
    def _():
        for h in range(DA_HEADS):
            _online_softmax_step(scores(h), v_ref[:, h * DA_V:(h + 1) * DA_V], m_s.at[h], l_s.at[h], acc_s.at[h])

    @pl.when(ki == qi)
    def _():
        r = lax.broadcasted_iota(I32, (2 * blk, blk), 0) % blk
        cidx = lax.broadcasted_iota(I32, (2 * blk, blk), 1)
        lam = _lambda(lamv_ref)
        for h in range(DA_HEADS):
            _online_softmax_step(jnp.where(cidx <= r, scores(h), -jnp.inf), v_ref[:, h * DA_V:(h + 1) * DA_V],
                                 m_s.at[h], l_s.at[h], acc_s.at[h])
            o2 = acc_s[h] / l_s[h]
            o = o2[0:blk, :] - lam * o2[blk:2 * blk, :]
            o_ref[:, h * DA_V:(h + 1) * DA_V] = _subln(o, gsub_ref).astype(BF16)


def _attn_prompt(qa, ktb, vab, lamv, gsub, *, batch, seq, blk):
    nq = seq // blk
    pairs = [(qi, ki) for qi in range(nq) for ki in range(qi + 1)]
    qt = jnp.asarray([p[0] for p in pairs], I32)
    kt = jnp.asarray([p[1] for p in pairs], I32)
    hw = 2 * DA_QK
    width = DA_HEADS * hw

    def q_idx(b, p, qt, kt):
        return (b * nq + qt[p], 0)

    def k_idx(b, p, qt, kt):
        return (b, kt[p])

    def v_idx(b, p, qt, kt):
        return (b * nq + kt[p], 0)

    def const(b, p, qt, kt):
        return (0, 0)

    grid_spec = pltpu.PrefetchScalarGridSpec(
        num_scalar_prefetch=2, grid=(batch, len(pairs)),
        in_specs=[pl.BlockSpec((blk, width), q_idx), pl.BlockSpec((width, blk), k_idx),
                  pl.BlockSpec((blk, DA_HEADS * DA_V), v_idx),
                  pl.BlockSpec(lamv.shape, const), pl.BlockSpec((1, DA_V), const)],
        out_specs=pl.BlockSpec((blk, DA_HEADS * DA_V), q_idx),
        scratch_shapes=[pltpu.VMEM((DA_HEADS, 2 * blk, hw), BF16), pltpu.VMEM((DA_HEADS, 2 * blk, LANES), F32),
                        pltpu.VMEM((DA_HEADS, 2 * blk, LANES), F32), pltpu.VMEM((DA_HEADS, 2 * blk, DA_V), F32)])
    return pl.pallas_call(
        functools.partial(_attn_kernel, blk=blk), grid_spec=grid_spec,
        out_shape=jax.ShapeDtypeStruct((batch * seq, DA_HEADS * DA_V), BF16),
        compiler_params=_params("arbitrary", "arbitrary"), name="attn_prompt",
    )(qt, kt, qa, ktb, vab, lamv, gsub)


def _dec_attn_kernel(pt_ref, q_ref, k_hbm, v_hbm, kn_ref, vn_ref, lamv_ref, gsub_ref, o_ref,
                     kbuf, vbuf, m_s, l_s, acc_s, sem_k, sem_v, *, pages, page, n_new):
    b = pl.program_id(0)
    j = pl.program_id(1)
    nj = pl.num_programs(1)
    t = b * nj + j
    n_steps = pl.num_programs(0) * nj
    krows = DA_HEADS * 2 * DA_QK
    vrows = page * DA_HEADS

    def k_copy(step, p):
        pg = pt_ref[step // nj, (step % nj) * pages + p]
        return pltpu.make_async_copy(k_hbm.at[pl.ds(pl.multiple_of(pg * krows, krows), krows)],
                                     kbuf.at[step % 3, pl.ds(p * krows, krows)], sem_k.at[step % 3])

    def v_copy(step, p):
        pg = pt_ref[step // nj, (step % nj) * pages + p]
        return pltpu.make_async_copy(v_hbm.at[pl.ds(pl.multiple_of(pg * vrows, vrows), vrows)],
                                     vbuf.at[step % 3, pl.ds(p * vrows, vrows)], sem_v.at[step % 3])

    def fetch(step):
        for p in range(pages):
            k_copy(step, p).start()
            v_copy(step, p).start()

    @pl.when(t == 0)
    def _():
        fetch(0)

        @pl.when(n_steps > 1)
        def _():
            fetch(1)

    @pl.when(t + 2 < n_steps)
    def _():
        fetch(t + 2)

    slot = t % 3
    pltpu.make_async_copy(k_hbm.at[pl.ds(0, pages * krows)], kbuf.at[slot], sem_k.at[slot]).wait()
    pltpu.make_async_copy(v_hbm.at[pl.ds(0, pages * vrows)], vbuf.at[slot], sem_v.at[slot]).wait()

    @pl.when(j == 0)
    def _():
        m_s[...] = jnp.full(m_s.shape, -jnp.inf, F32)
        l_s[...] = jnp.zeros(l_s.shape, F32)
        acc_s[...] = jnp.zeros(acc_s.shape, F32)

    half_rows = DA_HEADS * SUBLANES
    q = q_ref[...]

    def k_page(p):
        return kbuf[slot, p * krows:(p + 1) * krows, :].astype(BF16)

    def v_page(p):
        return jnp.concatenate([vbuf[slot, pl.ds(p * vrows + h, page, stride=DA_HEADS), :] for h in range(DA_HEADS)],
                               axis=1).astype(BF16)

    s = jnp.concatenate([_dot(q, k_page(p)) for p in range(pages)], axis=1)
    m_prev = m_s[...]
    m_new = jnp.maximum(m_prev, jnp.max(s, axis=-1, keepdims=True))
    alpha = jnp.exp2(m_prev - m_new)
    pr = jnp.exp2(s - _lanes(m_new, s.shape[1]))
    pv = sum(_dot(pr[:, p * page:(p + 1) * page].astype(BF16), v_page(p)) for p in range(pages))
    l_s[...] = alpha * l_s[...] + jnp.sum(pr, axis=-1, keepdims=True)
    acc_s[...] = _lanes(alpha, acc_s.shape[1]) * acc_s[...] + pv
    m_s[...] = m_new

    @pl.when(j == nj - 1)
    def _():
        sn = _dot(q, kn_ref[...])
        tt = jnp.minimum(lax.broadcasted_iota(I32, sn.shape, 0) % SUBLANES, n_new - 1)
        cidx = lax.broadcasted_iota(I32, sn.shape, 1)
        _online_softmax_step(jnp.where(cidx <= tt, sn, -jnp.inf), vn_ref[...], m_s, l_s, acc_s)
        o2 = acc_s[...] / _lanes(l_s[...], acc_s.shape[1])
        lam = _lambda(lamv_ref)
        outs = []
        for h in range(DA_HEADS):
            r0 = h * SUBLANES
            o0 = o2[r0:r0 + SUBLANES, h * DA_V:(h + 1) * DA_V]
            o1 = o2[half_rows + r0:half_rows + r0 + SUBLANES, h * DA_V:(h + 1) * DA_V]
            outs.append(_subln(o0 - lam * o1, gsub_ref))
        o_ref[...] = jnp.concatenate(outs, axis=1)


def _attn_sample(page_table, qbd, cache_kt, cache_v2, knt, vn, lamv, gsub, *, n_new, page):
    bs, npg = page_table.shape
    pages = min(PAGES_PER_STEP, npg)
    while npg % pages:
        pages -= 1
    rows, width = qbd.shape[1], qbd.shape[2]
    vrows = page * DA_HEADS

    def seq3(b, j, pt):
        return (b, 0, 0)

    def const(b, j, pt):
        return (0, 0)

    in_specs = [pl.BlockSpec((None, rows, width), seq3),
                pl.BlockSpec(memory_space=pl.ANY), pl.BlockSpec(memory_space=pl.ANY),
                pl.BlockSpec((None, width, NEW_KV_PAD), seq3), pl.BlockSpec((None, NEW_KV_PAD, width), seq3),
                pl.BlockSpec(lamv.shape, const), pl.BlockSpec((1, DA_V), const)]
    grid_spec = pltpu.PrefetchScalarGridSpec(
        num_scalar_prefetch=1, grid=(bs, npg // pages), in_specs=in_specs,
        out_specs=pl.BlockSpec((None, SUBLANES, width), seq3),
        scratch_shapes=[pltpu.VMEM((3, pages * width, page), F32), pltpu.VMEM((3, pages * vrows, DA_V), F32),
                        pltpu.VMEM((rows, LANES), F32), pltpu.VMEM((rows, LANES), F32), pltpu.VMEM((rows, width), F32),
                        pltpu.SemaphoreType.DMA((3,)), pltpu.SemaphoreType.DMA((3,))])
    return pl.pallas_call(
        functools.partial(_dec_attn_kernel, pages=pages, page=page, n_new=n_new), grid_spec=grid_spec,
        out_shape=jax.ShapeDtypeStruct((bs, SUBLANES, width), F32),
        compiler_params=_params("arbitrary", "arbitrary"), name="attn_sample",
    )(page_table, qbd, cache_kt, cache_v2, knt, vn, lamv, gsub)


def _merge_kernel(xp_ref, xs_ref, hgp_ref, hgs_ref, op_ref, os_ref, sga_ref, sgb_ref, wa_ref, wb_ref, wo_ref,
                  gffn_ref, wr_ref, br_ref, x1_ref, lg_ref, *, n_prompt_tiles):
    is_prompt = pl.program_id(0) < n_prompt_tiles
    x = jnp.where(is_prompt, xp_ref[...], xs_ref[...])
    hg = jnp.where(is_prompt, hgp_ref[...], hgs_ref[...])
    o = jnp.where(is_prompt, op_ref[...], os_ref[...])
    mixed = sga_ref[...].astype(F32) * _dot(hg, wa_ref[...]) + sgb_ref[...].astype(F32) * _dot(o, wb_ref[...])
    x1 = x + _dot(mixed.astype(BF16), wo_ref[...])
    _slab_store(x1_ref, x1, x1.shape[1] // LANES)
    xn = (_rms(x1, NORM_EPS) * gffn_ref[...]).astype(BF16)
    lg_ref[...] = _dot_nt(wr_ref[...], xn) + br_ref[...]


def _merge(x_p, x_s, hg_p, hg_s, o_p, o_s, sga, sgb, wa, wb, wo, g_ffn, wr, br, tm):
    tp, d = x_p.shape
    ts = x_s.shape[0]
    npt, nst = tp // tm, ts // tm
    t_all = tp + ts

    def tok(i):
        return (i, 0)

    def const(i):
        return (0, 0)

    def p_idx(i):
        return (jnp.minimum(i, npt - 1), 0)

    def s_idx(i):
        return (jnp.maximum(i - npt, 0), 0)

    wv, wo_in = hg_p.shape[1], o_p.shape[1]
    in_specs = [pl.BlockSpec((tm, d), p_idx), pl.BlockSpec((tm, d), s_idx),
                pl.BlockSpec((tm, wv), p_idx), pl.BlockSpec((tm, wv), s_idx),
                pl.BlockSpec((tm, wo_in), p_idx), pl.BlockSpec((tm, wo_in), s_idx),
                pl.BlockSpec((tm, d), tok), pl.BlockSpec((tm, d), tok),
                pl.BlockSpec(wa.shape, const), pl.BlockSpec(wb.shape, const), pl.BlockSpec(wo.shape, const),
                pl.BlockSpec((1, d), const), pl.BlockSpec(wr.shape, const), pl.BlockSpec((ROUTER_ROWS, 1), const)]
    ch = d // LANES
    out_shape = (jax.ShapeDtypeStruct((t_all * ch, LANES), F32), jax.ShapeDtypeStruct((ROUTER_ROWS, t_all), F32))
    out_specs = (pl.BlockSpec((tm * ch, LANES), tok), pl.BlockSpec((ROUTER_ROWS, tm), lambda i: (0, i)))
    return pl.pallas_call(
        functools.partial(_merge_kernel, n_prompt_tiles=npt),
        grid=(npt + nst,), in_specs=in_specs, out_specs=out_specs, out_shape=out_shape,
        compiler_params=_params("arbitrary"), name="merge",
    )(x_p, x_s, hg_p, hg_s, o_p, o_s, sga, sgb, wa, wb, wo, g_ffn, wr, br)


def _route_kernel(lg_ref, eid_ref, gw_ref):
    x = lg_ref[...]
    sub = lax.broadcasted_iota(I32, (SUBLANES, x.shape[1]), 0)
    lg = jnp.where(sub < MOE_GROUPS, x[0:SUBLANES, :], -jnp.inf)
    gmax = jnp.max(lg, axis=0, keepdims=True)
    g_star = jnp.min(jnp.where(lg == gmax, sub, SUBLANES), axis=0, keepdims=True)
    pg_top = 1.0 / jnp.sum(jnp.exp(lg - gmax), axis=0, keepdims=True)
    le = x[SUBLANES:2 * SUBLANES, :]
    for g in range(1, MOE_GROUPS):
        le = jnp.where(g_star == g, x[(g + 1) * SUBLANES:(g + 2) * SUBLANES, :], le)
    ex = jnp.exp(le - jnp.max(le, axis=0, keepdims=True))
    pe = ex / jnp.sum(ex, axis=0, keepdims=True)
    v1 = jnp.max(pe, axis=0, keepdims=True)
    i1 = jnp.min(jnp.where(pe == v1, sub, SUBLANES), axis=0, keepdims=True)
    rest = jnp.where(sub == i1, -jnp.inf, pe)
    v2 = jnp.max(rest, axis=0, keepdims=True)
    i2 = jnp.min(jnp.where(rest == v2, sub, SUBLANES), axis=0, keepdims=True)
    tot = v1 + v2
    e1 = g_star * MOE_PER_GROUP + i1
    e2 = g_star * MOE_PER_GROUP + i2
    w1 = pg_top * (v1 / tot)
    w2 = pg_top * (v2 / tot)
    eid_ref[...] = jnp.where(sub == 0, e1, jnp.where(sub == 1, e2, 0))
    gw_ref[...] = jnp.where(sub == 0, w1, jnp.where(sub == 1, w2, 0.0))


def _route(lg):
    rows, t_all = lg.shape
    tb = ROUTE_TILE
    while t_all % tb:
        tb //= 2
    spec = pl.BlockSpec((SUBLANES, tb), lambda i: (0, i))
    return pl.pallas_call(
        _route_kernel, grid=(t_all // tb,),
        in_specs=[pl.BlockSpec((rows, tb), lambda i: (0, i))], out_specs=(spec, spec),
        out_shape=(jax.ShapeDtypeStruct((SUBLANES, t_all), I32), jax.ShapeDtypeStruct((SUBLANES, t_all), F32)),
        compiler_params=_params("arbitrary"), name="route",
    )(lg)


def _expert_kernel(iblk_ref, iexp_ref, ilo_ref, ihi_ref, nit_ref, idx_hbm, x_hbm, w_ref, gffn_ref, wg_ref, wu_ref, wd_ref,
                   y_hbm, idx_s, xbuf, ybuf, wg_s, wu_s, wd_s, sem_i, sem_g, sem_s, *, rows, n_blocks, chunks):
    i = pl.program_id(0)
    valid = i < nit_ref[0]
    blk = iblk_ref[i]
    lo = ilo_ref[i]
    hi = ihi_ref[i]
    first = lo == 0
    last = hi == rows
    xs = blk % 2

    def idx_copy(b):
        return pltpu.make_async_copy(idx_hbm.at[b], idx_s.at[pl.ds((b % 3) * 2 * rows, 2 * rows)], sem_i.at[b % 3])

    def issue_gather(b):
        slot = b % 2
        base = (b % 3) * 2 * rows

        def body(r, carry):
            src = pl.multiple_of(idx_s[base + r] * chunks, chunks)
            pltpu.make_async_copy(x_hbm.at[pl.ds(src, chunks)], xbuf.at[slot, pl.ds(pl.multiple_of(r * chunks, chunks), chunks)],
                                  sem_g.at[slot]).start()
            return carry

        lax.fori_loop(0, rows, body, 0, unroll=DMA_ISSUE_UNROLL)

    def wait_gather(slot):
        pltpu.make_async_copy(x_hbm.at[pl.ds(0, rows * chunks)], xbuf.at[slot], sem_g.at[slot]).wait()

    def issue_scatter(b):
        slot = b % 2
        base = (b % 3) * 2 * rows + rows

        def body(r, carry):
            dst = pl.multiple_of(idx_s[base + r] * chunks, chunks)
            pltpu.make_async_copy(ybuf.at[slot, pl.ds(pl.multiple_of(r * chunks, chunks), chunks)], y_hbm.at[pl.ds(dst, chunks)],
                                  sem_s.at[slot]).start()
            return carry

        lax.fori_loop(0, rows, body, 0, unroll=DMA_ISSUE_UNROLL)

    def wait_scatter(slot):
        pltpu.make_async_copy(ybuf.at[slot], y_hbm.at[pl.ds(0, rows * chunks)], sem_s.at[slot]).wait()

    @pl.when(jnp.logical_and(valid, i == 0))
    def _():
        idx_copy(0).start()
        idx_copy(0).wait()
        if n_blocks > 1:
            idx_copy(1).start()
        issue_gather(0)

    @pl.when(jnp.logical_and(valid, first))
    def _():
        wait_gather(xs)

        @pl.when(blk + 1 < n_blocks)
        def _():
            idx_copy(blk + 1).wait()
            issue_gather(blk + 1)

        @pl.when(blk + 2 < n_blocks)
        def _():
            idx_copy(blk + 2).start()

        @pl.when(blk >= 2)
        def _():
            wait_scatter(xs)

    changed = jnp.logical_or(i == 0, iexp_ref[i] != iexp_ref[jnp.maximum(i - 1, 0)])

    @pl.when(jnp.logical_and(valid, changed))
    def _():
        wg_s[...] = wg_ref[...].astype(BF16)
        wu_s[...] = wu_ref[...].astype(BF16)
        wd_s[...] = wd_ref[...].astype(BF16)

    def compute():
        xn = (_rms(_slab_load(xbuf, rows, chunks, lead=xs), NORM_EPS) * gffn_ref[...]).astype(BF16)
        g = _dot(xn, wg_s[...])
        u = _dot(xn, wu_s[...])
        hmid = (g * _sigmoid(g) * u).astype(BF16)
        r = lax.broadcasted_iota(I32, (rows, 1), 0)
        wrow = jnp.where(jnp.logical_and(r >= lo, r < hi), w_ref[...], 0.0)
        return _dot(hmid, wd_s[...]) * wrow

    @pl.when(jnp.logical_and(valid, first))
    def _():
        _slab_store(ybuf, compute(), chunks, lead=xs)

    @pl.when(jnp.logical_and(valid, jnp.logical_not(first)))
    def _():
        _slab_store(ybuf, _slab_load(ybuf, rows, chunks, lead=xs) + compute(), chunks, lead=xs)

    @pl.when(jnp.logical_and(valid, last))
    def _():
        issue_scatter(blk)

    @pl.when(jnp.logical_and(valid, i == nit_ref[0] - 1))
    def _():
        wait_scatter(xs)

        @pl.when(blk >= 1)
        def _():
            wait_scatter(1 - xs)


def _experts(item_blk, item_exp, item_lo, item_hi, n_items, idx_rows, x1_slab, w_rows, g_ffn, w_gate, w_up, w_down, rows):
    n_blocks = idx_rows.shape[0]
    d, ff = w_gate.shape[-2], w_gate.shape[-1]
    chunks = d // LANES
    n_out = MOE_TOP_K * x1_slab.shape[0]
    n_max = item_blk.shape[0]

    def blk_idx(i, ib, ie, il, ih, nt):
        return (ib[i], 0)

    def const(i, ib, ie, il, ih, nt):
        return (0, 0)

    def w_idx(i, ib, ie, il, ih, nt):
        return (ie[i], 0, 0)

    grid_spec = pltpu.PrefetchScalarGridSpec(
        num_scalar_prefetch=5, grid=(n_max,),
        in_specs=[pl.BlockSpec(memory_space=pl.ANY), pl.BlockSpec(memory_space=pl.ANY),
                  pl.BlockSpec((rows, 1), blk_idx), pl.BlockSpec((1, d), const),
                  pl.BlockSpec((None, d, ff), w_idx), pl.BlockSpec((None, d, ff), w_idx),
                  pl.BlockSpec((None, ff, d), w_idx)],
        out_specs=pl.BlockSpec(memory_space=pl.ANY),
        scratch_shapes=[pltpu.SMEM((3 * 2 * rows,), I32),
                        pltpu.VMEM((2, rows * chunks, LANES), F32), pltpu.VMEM((2, rows * chunks, LANES), F32),
                        pltpu.VMEM((d, ff), BF16), pltpu.VMEM((d, ff), BF16), pltpu.VMEM((ff, d), BF16),
                        pltpu.SemaphoreType.DMA((3,)), pltpu.SemaphoreType.DMA((2,)), pltpu.SemaphoreType.DMA((2,))])
    return pl.pallas_call(
        functools.partial(_expert_kernel, rows=rows, n_blocks=n_blocks, chunks=chunks), grid_spec=grid_spec,
        out_shape=jax.ShapeDtypeStruct((n_out, LANES), F32),
        compiler_params=_params("arbitrary"), name="experts",
    )(item_blk, item_exp, item_lo, item_hi, n_items, idx_rows, x1_slab, w_rows, g_ffn, w_gate, w_up, w_down)


def _final_kernel(x1_ref, y0_ref, y1_ref, plep_ref, ples_ref, gple_ref, wpg_ref, wpp_ref, gfin_ref, yp_ref, ys_ref,
                  *, n_prompt_tiles):
    is_prompt = pl.program_id(0) < n_prompt_tiles
    tm, ch = plep_ref.shape[0], gple_ref.shape[1] // LANES
    x2 = _slab_load(x1_ref, tm, ch) + (_slab_load(y0_ref, tm, ch) + _slab_load(y1_ref, tm, ch))
    xn = (_rms(x2, NORM_EPS) * gple_ref[...]).astype(BF16)
    ple = jnp.where(is_prompt, plep_ref[...], ples_ref[...]).astype(BF16)
    x3 = x2 + _sigmoid(_dot(xn, wpg_ref[...])) * _dot(ple, wpp_ref[...])
    y = _rms(x3, NORM_EPS) * gfin_ref[...]

    @pl.when(is_prompt)
    def _():
        yp_ref[...] = y

    @pl.when(jnp.logical_not(is_prompt))
    def _():
        ys_ref[...] = y


def _final(x1, y_slots, ple_p, ple_s, g_ple, wpg, wpp, g_final, tm):
    d = g_ple.shape[1]
    ch = d // LANES
    tp, ts = ple_p.shape[0], ple_s.shape[0]
    npt, nst = tp // tm, ts // tm
    pd = ple_p.shape[1]
    nt = npt + nst

    def tok(i):
        return (i, 0)

    def tok1(i):
        return (nt + i, 0)

    def const(i):
        return (0, 0)

    def p_idx(i):
        return (jnp.minimum(i, npt - 1), 0)

    def s_idx(i):
        return (jnp.maximum(i - npt, 0), 0)

    in_specs = [pl.BlockSpec((tm * ch, LANES), tok), pl.BlockSpec((tm * ch, LANES), tok),
                pl.BlockSpec((tm * ch, LANES), tok1),
                pl.BlockSpec((tm, pd), p_idx), pl.BlockSpec((tm, pd), s_idx),
                pl.BlockSpec((1, d), const), pl.BlockSpec(wpg.shape, const), pl.BlockSpec(wpp.shape, const),
                pl.BlockSpec((1, d), const)]
    return pl.pallas_call(
        functools.partial(_final_kernel, n_prompt_tiles=npt),
        grid=(nt,), in_specs=in_specs,
        out_specs=(pl.BlockSpec((tm, d), p_idx), pl.BlockSpec((tm, d), s_idx)),
        out_shape=(jax.ShapeDtypeStruct((tp, d), F32), jax.ShapeDtypeStruct((ts, d), F32)),
        compiler_params=_params("arbitrary"), name="final",
    )(x1, y_slots, y_slots, ple_p, ple_s, g_ple, wpg, wpp, g_final)


def _rope_tables(pos):
    half = DA_QK // 2
    inv = ROPE_THETA ** (-jnp.arange(half, dtype=F32) / half)
    ang = pos.astype(F32)[:, None] * inv[None, :]
    cos, sin = jnp.cos(ang), jnp.sin(ang)
    reps = LANES // DA_QK
    cos_t = jnp.tile(jnp.concatenate([cos, cos], axis=1), (1, reps))
    sin_t = jnp.tile(jnp.concatenate([-sin, sin], axis=1), (1, reps))
    return cos_t, sin_t


def _tile(limit, *sizes):
    t = limit
    while any(s % t for s in sizes):
        t //= 2
    return t


def kernel(x_prompt, x_sample, cache_k, cache_v, state_mlstm_C, state_mlstm_n, state_mlstm_m, page_table, p_prompt, p_sample, g_mix, w_in, b_ml_i, b_ml_f, lam_q1, lam_k1, lam_q2, lam_k2, g_sub, w_br_a, w_br_b, w_out, g_ffn, w_rg, b_rg, w_re, b_re, w_e_gate, w_e_up, w_e_down, g_ple, w_ple_gate, w_ple_proj, g_final):
    depth = w_in.shape[0]
    assert depth == 1, "single-layer step"
    bp, sp, d = x_prompt.shape
    bs, ss, _ = x_sample.shape
    assert ss <= SUBLANES
    tp, ts = bp * sp, bs * ss
    t_all = tp + ts
    n_pages, page = page_table.shape[1], cache_k.shape[2]
    past_len = n_pages * page
    w = ML_HEADS * ML_DK
    aw = DA_HEADS * 2 * DA_QK
    li = 0

    wi = w_in[li]
    sizes = (w, w, ML_HEADS * ML_DV, ML_HEADS * ML_DV, ML_HEADS, ML_HEADS, aw, aw, DA_HEADS * DA_V, d, d)
    edges = [0]
    for n in sizes:
        edges.append(edges[-1] + n)
    assert edges[-1] == wi.shape[1]
    seg = [wi[:, edges[i]:edges[i + 1]] for i in range(11)]
    w_main = jnp.concatenate(seg[0:4] + [seg[6]] + seg[8:11], axis=1).astype(BF16)
    w_kt = seg[7].T.astype(BF16)
    w_gates = jnp.concatenate([seg[4], seg[5]], axis=1)
    w_gr = jnp.pad(w_gates.T, ((0, BF16_SUBLANES - 2 * ML_HEADS), (0, 0))).astype(BF16)
    w_gc = jnp.pad(w_gates, ((0, 0), (0, LANES - 2 * ML_HEADS))).astype(BF16)
    b_gates = jnp.concatenate([b_ml_i[li], b_ml_f[li]]).astype(F32)
    b_row = jnp.pad(b_gates, (0, BF16_SUBLANES - 2 * ML_HEADS))[:, None]
    b_col = jnp.pad(b_gates, (0, LANES - 2 * ML_HEADS))[None, :]
    lamv = jnp.stack([lam_q1[li], lam_k1[li], lam_q2[li], lam_k2[li]]).astype(F32)
    gsub = g_sub[li][None, :].astype(F32)
    w_router = jnp.zeros((ROUTER_ROWS, d), F32).at[0:MOE_GROUPS].set(w_rg[li].T).at[SUBLANES:SUBLANES + MOE_EXPERTS].set(w_re[li].T)
    b_router = jnp.zeros((ROUTER_ROWS,), F32).at[0:MOE_GROUPS].set(b_rg[li]).at[SUBLANES:SUBLANES + MOE_EXPERTS].set(b_re[li])

    tm = _tile(TOKEN_TILE, sp, ts)
    cos_p, sin_p = _rope_tables(jnp.arange(sp))
    cos_s, sin_s = _rope_tables(past_len + (jnp.arange(tm) % ss))
    cos_t = jnp.concatenate([cos_p, cos_s], axis=0)
    sin_t = jnp.concatenate([sin_p, sin_s], axis=0)
    cos_tt = cos_t[:, :DA_QK].T
    sin_tt = sin_t[:, :DA_QK].T
    xp2 = x_prompt.reshape(tp, d)
    xs2 = x_sample.reshape(ts, d)
    (q_ml, k_ml, v_ml, og, grow, gcol, qa, vab, sga, sgb, kt_p, ktb_p, v_p, kt_s, ktb_s, v_s) = _inproj(
        xp2, xs2, g_mix[li][None, :], w_main, w_kt, w_gr, w_gc, b_row, b_col, cos_t, sin_t, cos_tt, sin_tt, tm, sp)

    chunk = _tile(MLSTM_CHUNK, sp)
    ncp = sp // chunk
    grow3_p = grow[:, :tp].reshape(BF16_SUBLANES, bp * ncp, chunk).transpose(1, 0, 2)
    zc = jnp.zeros((bp, ML_HEADS, ML_DV, ML_DK), F32)
    zn = jnp.zeros((bp, ML_HEADS, ML_DK), F32)
    zm = jnp.zeros((bp, ML_HEADS, LANES), F32)
    hg_p, c_p, n_p, m_p = _mlstm(q_ml, k_ml, v_ml, og, grow3_p, gcol, zc, zn, zm,
                                 batch=bp, chunk=chunk, row_block_offset=0)

    padn = SAMPLE_PAD - ss

    def pad_seq(a):
        return jnp.pad(a[tp:].reshape(bs, ss, -1), ((0, 0), (0, padn), (0, 0))).reshape(bs * SAMPLE_PAD, -1)

    neutral = jnp.where(jnp.arange(LANES) < ML_HEADS, -1e30, 0.0).astype(F32)
    gcol_s = jnp.concatenate([gcol[tp:].reshape(bs, ss, LANES),
                              jnp.broadcast_to(neutral, (bs, padn, LANES))], axis=1).reshape(bs * SAMPLE_PAD, LANES)
    grow3_s = jnp.concatenate([grow[:, tp:].reshape(BF16_SUBLANES, bs, ss).transpose(1, 0, 2),
                               jnp.broadcast_to(neutral[:BF16_SUBLANES, None], (bs, BF16_SUBLANES, padn))], axis=2)
    m0_s = jnp.broadcast_to(state_mlstm_m[li].astype(F32)[:, :, None], (bs, ML_HEADS, LANES))
    hg_s_pad, c_s, n_s, m_s = _mlstm(pad_seq(q_ml), pad_seq(k_ml), pad_seq(v_ml), pad_seq(og), grow3_s, gcol_s,
                                     state_mlstm_C[li].astype(F32), state_mlstm_n[li].astype(F32), m0_s,
                                     batch=bs, chunk=SAMPLE_PAD, row_block_offset=0)
    hg_s = hg_s_pad.reshape(bs, SAMPLE_PAD, -1)[:, :ss].reshape(ts, -1)

    blk = _tile(ATTN_BLOCK, sp)
    o_p = _attn_prompt(qa, ktb_p, vab, lamv, gsub, batch=bp, seq=sp, blk=blk)

    q_s = jnp.pad(qa[tp:].reshape(bs, ss, DA_HEADS, 2, DA_QK), ((0, 0), (0, SUBLANES - ss), (0, 0), (0, 0), (0, 0)))
    q_cht = q_s.transpose(0, 3, 2, 1, 4)
    same = jnp.logical_and(
        (jnp.arange(DA_HEADS)[:, None] == jnp.arange(DA_HEADS)[None, :])[None, None, :, None, :, None, None],
        (jnp.arange(2)[:, None] == jnp.arange(2)[None, :])[None, :, None, None, None, :, None])
    qbd = jnp.where(same, q_cht[:, :, :, :, None, None, :], jnp.zeros((), BF16)).reshape(bs, 2 * DA_HEADS * SUBLANES, aw)
    knt = jnp.pad(ktb_s.reshape(aw, bs, ss).transpose(1, 0, 2), ((0, 0), (0, 0), (0, NEW_KV_PAD - ss)))
    vn = jnp.pad(vab[tp:].reshape(bs, ss, aw), ((0, 0), (0, NEW_KV_PAD - ss), (0, 0)))
    ckt = cache_k[li].transpose(0, 2, 3, 4, 1).reshape(-1, page)
    cv2 = cache_v[li].reshape(-1, DA_V)
    o_s = _attn_sample(page_table, qbd, ckt, cv2, knt, vn, lamv, gsub, n_new=ss, page=page)
    o_s = o_s[:, :ss].reshape(ts, aw).astype(BF16)

    x1, lg = _merge(xp2, xs2, hg_p, hg_s, o_p, o_s, sga, sgb,
                    w_br_a[li].astype(BF16), w_br_b[li].astype(BF16), w_out[li].astype(BF16),
                    g_ffn[li][None, :], w_router.astype(BF16), b_router[:, None], tm)

    eid8, gw8 = _route(lg)
    n_slots = t_all * MOE_TOP_K
    rows = _tile(EXPERT_ROWS, n_slots)
    nb = n_slots // rows
    flat_e = eid8[:MOE_TOP_K].T.reshape(-1)
    flat_w = gw8[:MOE_TOP_K].T.reshape(-1)
    slot_bits = max(1, (n_slots - 1).bit_length())
    assert MOE_EXPERTS << slot_bits < 2 ** 31
    key = (flat_e << slot_bits) | jnp.arange(n_slots, dtype=I32)
    sorted_key, sorted_w = lax.sort((key, flat_w), num_keys=1)
    order = sorted_key & ((1 << slot_bits) - 1)
    tok_rows = (order // MOE_TOP_K).reshape(nb, rows)
    dst_rows = ((order % MOE_TOP_K) * t_all + order // MOE_TOP_K).reshape(nb, rows)
    idx_rows = jnp.concatenate([tok_rows, dst_rows], axis=1)
    counts = jnp.sum(flat_e[None, :] == jnp.arange(MOE_EXPERTS, dtype=I32)[:, None], axis=1, dtype=I32)
    ends = jnp.cumsum(counts)
    starts = ends - counts
    first_blk = starts // rows
    n_e = jnp.where(counts > 0, (ends - 1) // rows - first_blk + 1, 0)
    item_end = jnp.cumsum(n_e)
    item_start = item_end - n_e
    n_items = item_end[-1:]
    n_max = nb + MOE_EXPERTS - 1
    it = jnp.minimum(jnp.arange(n_max, dtype=I32), n_items[0] - 1)
    item_exp = jnp.sum(item_end[None, :] <= it[:, None], axis=1, dtype=I32)
    item_blk = first_blk[item_exp] + it - item_start[item_exp]
    item_lo = jnp.maximum(starts[item_exp] - item_blk * rows, 0)
    item_hi = jnp.minimum(ends[item_exp] - item_blk * rows, rows)
    y_slots = _experts(item_blk, item_exp, item_lo, item_hi, n_items, idx_rows, x1, sorted_w[:, None],
                       g_ffn[li][None, :], w_e_gate[li], w_e_up[li], w_e_down[li], rows)

    pd = p_prompt.shape[-1]
    y_p, y_s = _final(x1, y_slots, p_prompt[li].reshape(tp, pd), p_sample[li].reshape(ts, pd), g_ple[li][None, :],
                      w_ple_gate[li].astype(BF16), w_ple_proj[li].astype(BF16), g_final[None, :], tm)

    return (y_p.reshape(bp, sp, d), y_s.reshape(bs, ss, d),
            kt_p.reshape(1, bp, DA_HEADS, 2, DA_QK, sp).transpose(0, 1, 5, 2, 3, 4), v_p.reshape(1, bp, sp, DA_HEADS, DA_V),
            c_p[None], n_p[None], m_p[None, :, :, 0],
            kt_s.reshape(1, DA_HEADS, 2, DA_QK, bs, ss).transpose(0, 4, 5, 1, 2, 3), v_s.reshape(1, bs, ss, DA_HEADS, DA_V),
            c_s[None], n_s[None], m_s[None, :, :, 0])
```
